```python
import jax, jax.numpy as jnp
from jax import lax
import numpy as np

D_MODEL = 1024
BATCH = 8
SEQ = 4096
DEPTH = 2

SB_HEADS = 8
SB_HEAD_DIM = 64
SB_WIDTH = SB_HEADS * SB_HEAD_DIM
SB_BLOCK = 128
SG_GROUPS = 8
SG_GROUP_DIM = 64
SG_WIDTH = SG_GROUPS * SG_GROUP_DIM
SG_CHUNK = 128
GLA_HEADS = 4
GLA_DK = 64
GLA_DV = 128
GLA_KW = GLA_HEADS * GLA_DK
GLA_VW = GLA_HEADS * GLA_DV
GLA_GATE_RANK = 16
GLA_GATE_NORM = 16.0
GLA_CHUNK = 64
D_FF = 2816
CONV_W = 3
N_BRANCH = 3
EPS = 1e-6

IN_SPLITS = (SB_WIDTH, SB_WIDTH, SB_WIDTH,
             SG_WIDTH, SG_WIDTH,
             GLA_KW, GLA_KW, GLA_VW, GLA_VW,
             GLA_GATE_RANK,
             N_BRANCH * D_MODEL)
IN_WIDTH = sum(IN_SPLITS)
SPLIT_POINTS = tuple(int(p) for p in np.cumsum(IN_SPLITS)[:-1])

kernel_name = "hybrid_sb_sgmlp_gla_convffn"


def rmsnorm(x, g):
    x32 = x.astype(jnp.float32)
    y = x32 * lax.rsqrt(jnp.mean(x32 * x32, axis=-1, keepdims=True) + EPS) * g.astype(jnp.float32)
    return y.astype(x.dtype)


def split_heads(t, n_heads):
    b, s, w = t.shape
    return t.reshape(b, s, n_heads, w // n_heads).transpose(0, 2, 1, 3)


def merge_heads(t):
    b, h, s, d = t.shape
    return t.transpose(0, 2, 1, 3).reshape(b, s, h * d)


def stick_breaking_attention(q, k, v):
    seq = q.shape[2]
    scale = SB_HEAD_DIM ** -0.5
    outs = []
    for i in range(seq // SB_BLOCK):
        end = (i + 1) * SB_BLOCK
        qb = q[:, :, i * SB_BLOCK:end].astype(jnp.float32)
        kb = k[:, :, :end].astype(jnp.float32)
        vb = v[:, :, :end].astype(jnp.float32)
        z = jnp.einsum('bhtd,bhsd->bhts', qb, kb) * scale
        t_pos = i * SB_BLOCK + jnp.arange(SB_BLOCK)
        s_pos = jnp.arange(end)
        mask = s_pos[None, :] < t_pos[:, None]
        log_1m = jnp.where(mask, jax.nn.log_sigmoid(-z), 0.0)
        tail = lax.cumsum(log_1m, axis=3, reverse=True) - log_1m
        w = jnp.where(mask, jnp.exp(jax.nn.log_sigmoid(z) + tail), 0.0)
        outs.append(jnp.einsum('bhts,bhsd->bhtd', w, vb))
    return jnp.concatenate(outs, axis=2).astype(q.dtype)


def chunked_spatial_gating(u, v, ln_g, ln_b, w_s, b_s):
    b, s, _ = u.shape
    v32 = v.astype(jnp.float32)
    mu = jnp.mean(v32, axis=-1, keepdims=True)
    var = jnp.mean(jnp.square(v32 - mu), axis=-1, keepdims=True)
    vn = (v32 - mu) * lax.rsqrt(var + EPS) * ln_g.astype(jnp.float32) + ln_b.astype(jnp.float32)
    vn = vn.reshape(b, s // SG_CHUNK, SG_CHUNK, SG_GROUPS, SG_GROUP_DIM)
    causal = jnp.tril(jnp.ones((SG_CHUNK, SG_CHUNK), dtype=bool))
    w = jnp.where(causal[None], w_s.astype(jnp.float32), 0.0)
    sp = jnp.einsum('gts,bnsgd->bntgd', w, vn) + b_s.astype(jnp.float32).T[None, None, :, :, None]
    return (u.astype(jnp.float32) * sp.reshape(b, s, SG_WIDTH)).astype(u.dtype)


def gla_chunked(q, k, v, log_a):
    b, h, s, dk = q.shape
    dv = v.shape[-1]
    n, c = s // GLA_CHUNK, GLA_CHUNK
    q, k, log_a = (t.reshape(b, h, n, c, dk) for t in (q, k, log_a))
    v = v.reshape(b, h, n, c, dv)
    cum = jnp.cumsum(log_a, axis=3)
    cum_last = cum[:, :, :, -1:]
    q_dec = q * jnp.exp(cum)
    k_inv = k * jnp.exp(-cum)
    k_to_end = k * jnp.exp(cum_last - cum)
    causal = jnp.tril(jnp.ones((c, c), dtype=bool))
    attn = jnp.where(causal, jnp.einsum('bhntk,bhnsk->bhnts', q_dec, k_inv), 0.0)
    o_intra = jnp.einsum('bhnts,bhnsv->bhntv', attn, v)
    chunk_upd = jnp.einsum('bhnsk,bhnsv->bhnkv', k_to_end, v)
    chunk_dec = jnp.exp(cum_last[:, :, :, 0])

    def step(state, inp):
        dec, upd = inp
        return dec[..., None] * state + upd, state

    _, states_before = lax.scan(step, jnp.zeros((b, h, dk, dv), jnp.float32),
                                (jnp.moveaxis(chunk_dec, 2, 0), jnp.moveaxis(chunk_upd, 2, 0)))
    states_before = jnp.moveaxis(states_before, 0, 2)
    o_inter = jnp.einsum('bhntk,bhnkv->bhntv', q_dec, states_before)
    return (o_intra + o_inter).reshape(b, h, s, dv)


def hybrid_mixer(xn, w_in, sg_ln_g, sg_ln_b, sg_w, sg_b, gla_w_gup, gla_b_gate, gla_norm_g,
                 p_a, p_b, p_c, w_out):
    b, s, _ = xn.shape
    proj = xn @ w_in
    (sb_q, sb_k, sb_v, sg_u, sg_v, g_q, g_k, g_v, g_r, g_down, gates) = jnp.split(proj, SPLIT_POINTS, axis=-1)

    y_a = merge_heads(stick_breaking_attention(split_heads(sb_q, SB_HEADS), split_heads(sb_k, SB_HEADS),
                                               split_heads(sb_v, SB_HEADS)))
    y_b = chunked_spatial_gating(jax.nn.gelu(sg_u), jax.nn.gelu(sg_v), sg_ln_g, sg_ln_b, sg_w, sg_b)
    log_a = jax.nn.log_sigmoid((g_down @ gla_w_gup + gla_b_gate).astype(jnp.float32)) / GLA_GATE_NORM
    o = gla_chunked(split_heads(g_q.astype(jnp.float32), GLA_HEADS) * (GLA_DK ** -0.5),
                    split_heads(g_k.astype(jnp.float32), GLA_HEADS),
                    split_heads(g_v.astype(jnp.float32), GLA_HEADS),
                    split_heads(log_a, GLA_HEADS))
    o = o * lax.rsqrt(jnp.mean(o * o, axis=-1, keepdims=True) + EPS) * gla_norm_g.astype(jnp.float32)
    y_c = (merge_heads(o) * jax.nn.silu(g_r.astype(jnp.float32))).astype(xn.dtype)

    g = jax.nn.sigmoid(gates).reshape(b, s, N_BRANCH, D_MODEL)
    merged = g[:, :, 0] * (y_a @ p_a) + g[:, :, 1] * (y_b @ p_b) + g[:, :, 2] * (y_c @ p_c)
    return merged @ w_out


def conv_gated_ffn(xn, w_up, conv_w, conv_b, w_down):
    s = xn.shape[1]
    hid = xn @ w_up
    hp = jnp.pad(hid, ((0, 0), (CONV_W - 1, 0), (0, 0)))
    conv = conv_b + hp[:, 0:s] * conv_w[0]
    for j in range(1, CONV_W):
        conv = conv + hp[:, j:j + s] * conv_w[j]
    a, u = jnp.split(conv, 2, axis=-1)
    return (jax.nn.gelu(a, approximate=True) * u) @ w_down


def setup_inputs(seed: int = 0) -> dict:
    key = jax.random.key(seed)
    ks = jax.random.split(key, 24)
    f32 = jnp.float32
    nrm = lambda k, shape, scale: jax.random.normal(k, shape, f32) * scale
    gain = lambda k, shape: 1.0 + 0.02 * jax.random.normal(k, shape, f32)
    L = DEPTH
    return {
        "x": jax.random.normal(ks[0], (BATCH, SEQ, D_MODEL), f32),
        "mix_pre_g": gain(ks[1], (L, D_MODEL)),
        "mix_post_g": gain(ks[2], (L, D_MODEL)),
        "w_in": nrm(ks[3], (L, D_MODEL, IN_WIDTH), D_MODEL ** -0.5),
        "sg_ln_g": gain(ks[4], (L, SG_WIDTH)),
        "sg_ln_b": nrm(ks[5], (L, SG_WIDTH), 0.02),
        "sg_w": nrm(ks[6], (L, SG_GROUPS, SG_CHUNK, SG_CHUNK), SG_CHUNK ** -0.5),
        "sg_b": 1.0 + nrm(ks[7], (L, SG_GROUPS, SG_CHUNK), 0.01),
        "gla_w_gup": nrm(ks[8], (L, GLA_GATE_RANK, GLA_KW), GLA_GATE_RANK ** -0.5),
        "gla_b_gate": nrm(ks[9], (L, GLA_KW), 0.1),
        "gla_norm_g": gain(ks[10], (L, GLA_DV)),
        "p_a": nrm(ks[11], (L, SB_WIDTH, D_MODEL), SB_WIDTH ** -0.5),
        "p_b": nrm(ks[12], (L, SG_WIDTH, D_MODEL), SG_WIDTH ** -0.5),
        "p_c": nrm(ks[13], (L, GLA_VW, D_MODEL), GLA_VW ** -0.5),
        "w_out": nrm(ks[14], (L, D_MODEL, D_MODEL), D_MODEL ** -0.5),
        "ffn_pre_g": gain(ks[15], (L, D_MODEL)),
        "ffn_post_g": gain(ks[16], (L, D_MODEL)),
        "ffn_w_up": nrm(ks[17], (L, D_MODEL, 2 * D_FF), D_MODEL ** -0.5),
        "ffn_conv_w": nrm(ks[18], (L, CONV_W, 2 * D_FF), CONV_W ** -0.5),
        "ffn_conv_b": nrm(ks[19], (L, 2 * D_FF), 0.02),
        "ffn_w_down": nrm(ks[20], (L, D_FF, D_MODEL), D_FF ** -0.5),
    }


def reference(x, mix_pre_g, mix_post_g, w_in, sg_ln_g, sg_ln_b, sg_w, sg_b, gla_w_gup, gla_b_gate,
              gla_norm_g, p_a, p_b, p_c, w_out, ffn_pre_g, ffn_post_g, ffn_w_up, ffn_conv_w,
              ffn_conv_b, ffn_w_down):
    h = x
    for l in range(DEPTH):
        y = hybrid_mixer(rmsnorm(h, mix_pre_g[l]), w_in[l], sg_ln_g[l], sg_ln_b[l], sg_w[l], sg_b[l],
                         gla_w_gup[l], gla_b_gate[l], gla_norm_g[l], p_a[l], p_b[l], p_c[l], w_out[l])
        h = h + rmsnorm(y, mix_post_g[l])
        y = conv_gated_ffn(rmsnorm(h, ffn_pre_g[l]), ffn_w_up[l], ffn_conv_w[l], ffn_conv_b[l], ffn_w_down[l])
        h = h + rmsnorm(y, ffn_post_g[l])
    return h
```

```python
import functools

import jax
import jax.numpy as jnp
from jax import lax
from jax.experimental import pallas as pl
from jax.experimental.pallas import tpu as pltpu

F32 = jnp.float32
BF16 = jnp.bfloat16

D_MODEL = 1024
SB_HEADS, SB_HEAD_DIM, SB_WIDTH = 8, 64, 512
SG_GROUPS, SG_GROUP_DIM, SG_WIDTH, SG_CHUNK = 8, 64, 512, 128
GLA_HEADS, GLA_DK, GLA_DV, GLA_KW, GLA_VW = 4, 64, 128, 256, 512
GLA_GATE_RANK, GLA_GATE_NORM, GLA_CHUNK = 16, 16.0, 64
D_FF, CONV_W, N_BRANCH = 2816, 3, 3
EPS = 1e-6

LANE = 128
VMEM_LIMIT = 56 * 1024 * 1024

COL_GATES = 0
COL_SBQ = COL_GATES + N_BRANCH * D_MODEL
COL_SBK = COL_SBQ + SB_WIDTH
COL_SBV = COL_SBK + SB_WIDTH
COL_SGU = COL_SBV + SB_WIDTH
COL_SGV = COL_SGU + SG_WIDTH
COL_GQ = COL_SGV + SG_WIDTH
COL_GK = COL_GQ + GLA_KW
COL_GV = COL_GK + GLA_KW
COL_GR = COL_GV + GLA_VW
COL_GDOWN = COL_GR + GLA_VW
P_WIDTH = COL_GDOWN + LANE

ROW_TILE = 512
PROJ_N_CHUNK = 512
SB_BLOCK = 256
FFN_F_CHUNK = 256
FFN_HALO = 16


def _params(n_axes):
    return pltpu.CompilerParams(dimension_semantics=("arbitrary",) * n_axes, vmem_limit_bytes=VMEM_LIMIT)


def _resident(shape):
    zeros = (0,) * len(shape)
    return pl.BlockSpec(shape, lambda *_: zeros, pipeline_mode=pl.Buffered(1))


def _rms(x, g):
    return x * lax.rsqrt(jnp.mean(x * x, axis=-1, keepdims=True) + EPS) * g


def _log_sigmoid_pair(z):
    s = jnp.log(1.0 + jnp.exp(-jnp.abs(z)))
    ls = jnp.minimum(z, 0.0) - s
    return ls, ls - z


def _split_bf16(x):
    hi = x.astype(BF16)
    lo = (x - hi.astype(F32)).astype(BF16)
    return hi, lo


def _dot(a, b):
    return jnp.dot(a, b, preferred_element_type=F32)


def _dot_nt(a, b):
    return lax.dot_general(a, b, (((1,), (1,)), ((), ())), preferred_element_type=F32)


def _dot_tn(a, b):
    return lax.dot_general(a, b, (((0,), (0,)), ((), ())), preferred_element_type=F32)


def _proj_kernel(x_ref, g_ref, w_ref, o_ref):
    xn = _rms(x_ref[...], g_ref[...]).astype(BF16)
    width = o_ref.shape[1]
    for c0 in range(0, width, PROJ_N_CHUNK):
        c1 = min(c0 + PROJ_N_CHUNK, width)
        o_ref[:, c0:c1] = _dot(xn, w_ref[:, c0:c1]).astype(o_ref.dtype)


def _proj(h, g, w):
    t = h.shape[0]
    return pl.pallas_call(
        _proj_kernel,
        out_shape=jax.ShapeDtypeStruct((t, P_WIDTH), BF16),
        grid=(t // ROW_TILE,),
        in_specs=[pl.BlockSpec((ROW_TILE, D_MODEL), lambda i: (i, 0)),
                  _resident((1, D_MODEL)),
                  _resident((D_MODEL, P_WIDTH))],
        out_specs=pl.BlockSpec((ROW_TILE, P_WIDTH), lambda i: (i, 0)),
        compiler_params=_params(1),
        name="proj",
    )(h, g, w)


def _sb_kernel(q_ref, k_ref, v_ref, u_ref, o_ref):
    tq = q_ref.shape[1]
    i = pl.program_id(2)
    q = q_ref[0]
    lane = lax.broadcasted_iota(jnp.int32, q.shape, 1)
    q_heads = (jnp.where(lane < SB_HEAD_DIM, q, jnp.zeros_like(q)),
               jnp.where(lane >= SB_HEAD_DIM, q, jnp.zeros_like(q)))
    u = u_ref[...]
    row = lax.broadcasted_iota(jnp.int32, (tq, tq), 0)
    col = lax.broadcasted_iota(jnp.int32, (tq, tq), 1)
    causal = col < row

    def block(j, carry, diag):
        start = pl.multiple_of(j * tq, tq)
        kb = k_ref[0, pl.ds(start, tq), :]
        vb = v_ref[0, pl.ds(start, tq), :]
        new = []
        for qh, (o, c) in zip(q_heads, carry):
            z = _dot_nt(qh, kb)
            log_b, log_1mb = _log_sigmoid_pair(z)
            if diag:
                log_1mb = jnp.where(causal, log_1mb, 0.0)
            hi, lo = _split_bf16(log_1mb)
            tail = _dot(hi, u) + _dot(lo, u)
            w = jnp.exp(log_b + tail + c)
            if diag:
                w = jnp.where(causal, w, 0.0)
            o = o + _dot(w.astype(BF16), vb)
            c = c + jnp.sum(log_1mb, axis=1, keepdims=True)
            new.append((o, c))
        return tuple(new)

    init = tuple((jnp.zeros((tq, LANE), F32), jnp.zeros((tq, 1), F32)) for _ in q_heads)
    carry = block(i, init, True)
    carry = lax.fori_loop(0, i, lambda s, cr: block(i - 1 - s, cr, False), carry)
    o_ref[0] = jnp.where(lane < SB_HEAD_DIM, carry[0][0], carry[1][0]).astype(o_ref.dtype)


def _sb_attention(p3, u):
    b, s, _ = p3.shape
    tq = SB_BLOCK
    qb, kb, vb = COL_SBQ // LANE, COL_SBK // LANE, COL_SBV // LANE
    return pl.pallas_call(
        _sb_kernel,
        out_shape=jax.ShapeDtypeStruct((b, s, SB_WIDTH), BF16),
        grid=(b, SB_HEADS // 2, s // tq),
        in_specs=[pl.BlockSpec((1, tq, LANE), lambda bi, hp, i: (bi, i, qb + hp)),
                  pl.BlockSpec((1, s, LANE), lambda bi, hp, i: (bi, 0, kb + hp)),
                  pl.BlockSpec((1, s, LANE), lambda bi, hp, i: (bi, 0, vb + hp)),
                  _resident((tq, tq))],
        out_specs=pl.BlockSpec((1, tq, LANE), lambda bi, hp, i: (bi, i, hp)),
        compiler_params=_params(3),
        name="sb_attention",
    )(p3, p3, p3, u)


def _sg_kernel(u_ref, v_ref, lng_ref, lnb_ref, w_ref, bias_ref, o_ref):
    rows = u_ref.shape[0]
    u = jax.nn.gelu(u_ref[...].astype(F32))
    v = jax.nn.gelu(v_ref[...].astype(F32))
    mu = jnp.mean(v, axis=-1, keepdims=True)
    d = v - mu
    var = jnp.mean(d * d, axis=-1, keepdims=True)
    vn = (d * lax.rsqrt(var + EPS) * lng_ref[...] + lnb_ref[...]).astype(BF16)

    wrow = lax.broadcasted_iota(jnp.int32, (SG_CHUNK, 2 * SG_CHUNK), 0)
    wcol = lax.broadcasted_iota(jnp.int32, (SG_CHUNK, 2 * SG_CHUNK), 1) & (SG_CHUNK - 1)
    tril = wcol <= wrow
    ws = [jnp.where(tril, w_ref[p], jnp.zeros((SG_CHUNK, 2 * SG_CHUNK), BF16)) for p in range(SG_GROUPS // 2)]
    lane = lax.broadcasted_iota(jnp.int32, (SG_CHUNK, LANE), 1)
    first = lane < SG_GROUP_DIM
    zero = jnp.zeros((SG_CHUNK, LANE), BF16)
    bias = bias_ref[...]
    for c in range(rows // SG_CHUNK):
        r0 = c * SG_CHUNK
        vc = vn[r0:r0 + SG_CHUNK]
        cols = []
        for p in range(SG_GROUPS // 2):
            blk = vc[:, p * LANE:(p + 1) * LANE]
            rhs = jnp.concatenate([jnp.where(first, blk, zero), jnp.where(first, zero, blk)], axis=0)
            cols.append(_dot(ws[p], rhs))
        sp = jnp.concatenate(cols, axis=1) + bias
        o_ref[r0:r0 + SG_CHUNK, :] = (u[r0:r0 + SG_CHUNK] * sp).astype(o_ref.dtype)


def _spatial_gating(p2, ln_g, ln_b, w_pairs, bias):
    t = p2.shape[0]
    ub, vb = COL_SGU // SG_WIDTH, COL_SGV // SG_WIDTH
    return pl.pallas_call(
        _sg_kernel,
        out_shape=jax.ShapeDtypeStruct((t, SG_WIDTH), BF16),
        grid=(t // ROW_TILE,),
        in_specs=[pl.BlockSpec((ROW_TILE, SG_WIDTH), lambda i: (i, ub)),
                  pl.BlockSpec((ROW_TILE, SG_WIDTH), lambda i: (i, vb)),
                  _resident((1, SG_WIDTH)),
                  _resident((1, SG_WIDTH)),
                  _resident((SG_GROUPS // 2, SG_CHUNK, 2 * SG_CHUNK)),
                  _resident((SG_CHUNK, SG_WIDTH))],
        out_specs=pl.BlockSpec((ROW_TILE, SG_WIDTH), lambda i: (i, 0)),
        compiler_params=_params(1),
        name="spatial_gating",
    )(p2, p2, ln_g, ln_b, w_pairs, bias)


def _gla_kernel(q_ref, k_ref, v_ref, r_ref, dn_ref, wg_ref, bg_ref, ng_ref, o_ref, state_ref, *, steps_per_seq):
    rows = q_ref.shape[0]

    @pl.when(pl.program_id(0) % steps_per_seq == 0)
    def _():
        state_ref[...] = jnp.zeros_like(state_ref)

    gate = _dot(dn_ref[...], wg_ref[...]) + bg_ref[...]
    log_a = _log_sigmoid_pair(gate)[0] * (1.0 / GLA_GATE_NORM)

    c = GLA_CHUNK
    trow = lax.broadcasted_iota(jnp.int32, (c, c), 0)
    tcol = lax.broadcasted_iota(jnp.int32, (c, c), 1)
    causal = tcol <= trow
    ltri = jnp.where(causal, 1.0, 0.0).astype(BF16)
    klane = lax.broadcasted_iota(jnp.int32, (c, GLA_KW), 1)
    slane = lax.broadcasted_iota(jnp.int32, (GLA_DV, GLA_KW), 1)
    ng = ng_ref[...]

    state = state_ref[...]
    for n in range(rows // c):
        r0 = n * c
        hi, lo = _split_bf16(log_a[r0:r0 + c])
        cum = _dot(ltri, hi) + _dot(ltri, lo)
        cum_last = cum[c - 1:c, :]
        qf = q_ref[r0:r0 + c, :].astype(F32)
        kf = k_ref[r0:r0 + c, :].astype(F32)
        q_dec = (qf * jnp.exp(cum)).astype(BF16)
        k_inv = (kf * jnp.exp(-cum)).astype(BF16)
        k_end = (kf * jnp.exp(cum_last - cum)).astype(BF16)
        state_b = state.astype(BF16)
        upd = jnp.zeros((GLA_DV, GLA_KW), F32)
        for h in range(GLA_HEADS):
            in_head = (klane >= h * GLA_DK) & (klane < (h + 1) * GLA_DK)
            qh = jnp.where(in_head, q_dec, jnp.zeros_like(q_dec))
            attn = jnp.where(causal, _dot_nt(qh, k_inv), 0.0)
            vh = v_ref[r0:r0 + c, h * GLA_DV:(h + 1) * GLA_DV]
            o = _dot(attn.astype(BF16), vh) + _dot_nt(qh, state_b)
            o = _rms(o, ng)
            rh = r_ref[r0:r0 + c, h * GLA_DV:(h + 1) * GLA_DV].astype(F32)
            o_ref[r0:r0 + c, h * GLA_DV:(h + 1) * GLA_DV] = (o * (rh * jax.nn.sigmoid(rh))).astype(o_ref.dtype)
            s_head = (slane >= h * GLA_DK) & (slane < (h + 1) * GLA_DK)
            upd = jnp.where(s_head, _dot_tn(vh, k_end), upd)
        state = state * jnp.exp(cum_last) + upd
    state_ref[...] = state


def _gla(p2, w_gup, b_gate, norm_g, seq):
    t = p2.shape[0]
    kern = functools.partial(_gla_kernel, steps_per_seq=seq // ROW_TILE)
    return pl.pallas_call(
        kern,
        out_shape=jax.ShapeDtypeStruct((t, GLA_VW), BF16),
        grid=(t // ROW_TILE,),
        in_specs=[pl.BlockSpec((ROW_TILE, GLA_KW), lambda i: (i, COL_GQ // GLA_KW)),
                  pl.BlockSpec((ROW_TILE, GLA_KW), lambda i: (i, COL_GK // GLA_KW)),
                  pl.BlockSpec((ROW_TILE, GLA_VW), lambda i: (i, COL_GV // GLA_VW)),
                  pl.BlockSpec((ROW_TILE, GLA_VW), lambda i: (i, COL_GR // GLA_VW)),
                  pl.BlockSpec((ROW_TILE, LANE), lambda i: (i, COL_GDOWN // LANE)),
                  _resident((LANE, GLA_KW)),
                  _resident((1, GLA_KW)),
                  _resident((1, GLA_DV))],
        out_specs=pl.BlockSpec((ROW_TILE, GLA_VW), lambda i: (i, 0)),
        scratch_shapes=[pltpu.VMEM((GLA_DV, GLA_KW), F32)],
        compiler_params=_params(1),
        name="gla",
    )(p2, p2, p2, p2, p2, w_gup, b_gate, norm_g)


def _merge_kernel(ya_ref, yb_ref, yc_ref, g_ref, h_ref, pa_ref, pb_ref, pc_ref, wo_ref, pg_ref, o_ref):
    merged = None
    for n, (y_ref, p_ref) in enumerate(((ya_ref, pa_ref), (yb_ref, pb_ref), (yc_ref, pc_ref))):
        gate = jax.nn.sigmoid(g_ref[:, n * D_MODEL:(n + 1) * D_MODEL].astype(F32))
        term = gate * _dot(y_ref[...], p_ref[...])
        merged = term if merged is None else merged + term
    y = _dot(merged.astype(BF16), wo_ref[...])
    o_ref[...] = h_ref[...] + _rms(y, pg_ref[...])


def _merge(ya, yb, yc, p2, h, p_a, p_b, p_c, w_out, post_g):
    t = h.shape[0]
    branch = pl.BlockSpec((ROW_TILE, SB_WIDTH), lambda i: (i, 0))
    return pl.pallas_call(
        _merge_kernel,
        out_shape=jax.ShapeDtypeStruct((t, D_MODEL), F32),
        grid=(t // ROW_TILE,),
        in_specs=[branch, branch, branch,
                  pl.BlockSpec((ROW_TILE, N_BRANCH * D_MODEL), lambda i: (i, 0)),
                  pl.BlockSpec((ROW_TILE, D_MODEL), lambda i: (i, 0)),
                  _resident((SB_WIDTH, D_MODEL)), _resident((SG_WIDTH, D_MODEL)), _resident((GLA_VW, D_MODEL)),
                  _resident((D_MODEL, D_MODEL)), _resident((1, D_MODEL))],
        out_specs=pl.BlockSpec((ROW_TILE, D_MODEL), lambda i: (i, 0)),
        compiler_params=_params(1),
        name="merge",
    )(ya, yb, yc, p2, h, p_a, p_b, p_c, w_out, post_g)


def _ffn_kernel(halo_ref, h_ref, g1_ref, wup_ref, cw_ref, cb_ref, wdn_ref, g2_ref, o_ref, act_ref, *, seq):
    tm = h_ref.shape[0]
    g1 = g1_ref[...]
    x = h_ref[...]
    xn = _rms(x, g1).astype(BF16)
    seq_start = (pl.program_id(0) * tm) % seq == 0
    xh = _rms(halo_ref[...], g1).astype(BF16)
    xh = jnp.where(seq_start, jnp.zeros_like(xh), xh)
    xe = jnp.concatenate([xh, xn], axis=0)
    fc = FFN_F_CHUNK
    for c in range(D_FF // fc):
        cols = slice(2 * fc * c, 2 * fc * (c + 1))
        hid = _dot(xe, wup_ref[:, cols])
        conv = cb_ref[:, cols]
        for j in range(CONV_W):
            off = FFN_HALO - (CONV_W - 1) + j
            conv = conv + hid[off:off + tm] * cw_ref[j:j + 1, cols]
        act = jax.nn.gelu(conv[:, :fc], approximate=True) * conv[:, fc:]
        act_ref[:, c * fc:(c + 1) * fc] = act.astype(BF16)
    y = _dot(act_ref[...], wdn_ref[...])
    o_ref[...] = x + _rms(y, g2_ref[...])


def _ffn(h, pre_g, w_up, conv_w, conv_b, w_down, post_g, seq):
    t = h.shape[0]
    halo_blocks = ROW_TILE // FFN_HALO
    kern = functools.partial(_ffn_kernel, seq=seq)
    return pl.pallas_call(
        kern,
        out_shape=jax.ShapeDtypeStruct((t, D_MODEL), F32),
        grid=(t // ROW_TILE,),
        in_specs=[pl.BlockSpec((FFN_HALO, D_MODEL), lambda i: (jnp.maximum(i * halo_blocks - 1, 0), 0)),
                  pl.BlockSpec((ROW_TILE, D_MODEL), lambda i: (i, 0)),
                  _resident((1, D_MODEL)),
                  _resident((D_MODEL, 2 * D_FF)),
                  _resident((CONV_W, 2 * D_FF)),
                  _resident((1, 2 * D_FF)),
                  _resident((D_FF, D_MODEL)),
                  _resident((1, D_MODEL))],
        out_specs=pl.BlockSpec((ROW_TILE, D_MODEL), lambda i: (i, 0)),
        scratch_shapes=[pltpu.VMEM((ROW_TILE, D_FF), BF16)],
        compiler_params=_params(1),
        name="ffn",
    )(h, h, pre_g, w_up, conv_w, conv_b, w_down, post_g)


def _interleave_ffn(a):
    lead = a.shape[:-1]
    a = a.reshape(lead + (2, D_FF // FFN_F_CHUNK, FFN_F_CHUNK))
    return jnp.swapaxes(a, -3, -2).reshape(lead + (2 * D_FF,))


def _prep_w_in(w_in):
    sbq, rest1, gq, rest2, gates = jnp.split(
        w_in, (SB_WIDTH, 3 * SB_WIDTH + 2 * SG_WIDTH, 3 * SB_WIDTH + 2 * SG_WIDTH + GLA_KW,
               3 * SB_WIDTH + 2 * SG_WIDTH + 2 * GLA_KW + 2 * GLA_VW + GLA_GATE_RANK), axis=-1)
    pad = jnp.zeros(w_in.shape[:-1] + (LANE - GLA_GATE_RANK,), w_in.dtype)
    return jnp.concatenate([gates, sbq * SB_HEAD_DIM ** -0.5, rest1, gq * GLA_DK ** -0.5, rest2, pad],
                           axis=-1).astype(BF16)


def kernel(x, mix_pre_g, mix_post_g, w_in, sg_ln_g, sg_ln_b, sg_w, sg_b, gla_w_gup, gla_b_gate, gla_norm_g,
           p_a, p_b, p_c, w_out, ffn_pre_g, ffn_post_g, ffn_w_up, ffn_conv_w, ffn_conv_b, ffn_w_down):
    b, s, d = x.shape
    depth = w_in.shape[0]
    assert d == D_MODEL and s % ROW_TILE == 0 and s % SB_BLOCK == 0

    w_in_p = _prep_w_in(w_in)
    sg_pairs = sg_w.reshape(depth, SG_GROUPS // 2, 2, SG_CHUNK, SG_CHUNK)
    sg_pairs = jnp.swapaxes(sg_pairs, 2, 3).reshape(depth, SG_GROUPS // 2, SG_CHUNK, 2 * SG_CHUNK).astype(BF16)
    sg_bias = jnp.repeat(jnp.swapaxes(sg_b, 1, 2), SG_GROUP_DIM, axis=2)
    w_gup_p = jnp.pad(gla_w_gup, ((0, 0), (0, LANE - GLA_GATE_RANK), (0, 0))).astype(BF16)
    w_up_r = _interleave_ffn(ffn_w_up).astype(BF16)
    conv_w_r = _interleave_ffn(ffn_conv_w)
    conv_b_r = _interleave_ffn(ffn_conv_b)[:, None, :]
    tri = jnp.arange(SB_BLOCK)
    u_strict = (tri[:, None] > tri[None, :]).astype(BF16)

    row = lambda a, l: a[l][None, :]
    h = x.reshape(b * s, d)
    for l in range(depth):
        p2 = _proj(h, row(mix_pre_g, l), w_in_p[l])
        ya = _sb_attention(p2.reshape(b, s, P_WIDTH), u_strict).reshape(b * s, SB_WIDTH)
        yb = _spatial_gating(p2, row(sg_ln_g, l), row(sg_ln_b, l), sg_pairs[l], sg_bias[l])
        yc = _gla(p2, w_gup_p[l], row(gla_b_gate, l), row(gla_norm_g, l), s)
        h = _merge(ya, yb, yc, p2, h, p_a[l].astype(BF16), p_b[l].astype(BF16), p_c[l].astype(BF16),
                   w_out[l].astype(BF16), row(mix_post_g, l))
        h = _ffn(h, row(ffn_pre_g, l), w_up_r[l], conv_w_r[l], conv_b_r[l], ffn_w_down[l].astype(BF16),
                 row(ffn_post_g, l), s)
    return h.reshape(b, s, d)
```

```python
import functools

import jax
import jax.numpy as jnp
from jax import lax
from jax.experimental import pallas as pl
from jax.experimental.pallas import tpu as pltpu

F32 = jnp.float32
BF16 = jnp.bfloat16

D_MODEL = 1024
SB_HEADS, SB_HEAD_DIM, SB_WIDTH = 8, 64, 512
SG_GROUPS, SG_GROUP_DIM, SG_WIDTH, SG_CHUNK = 8, 64, 512, 128
GLA_HEADS, GLA_DK, GLA_DV, GLA_KW, GLA_VW = 4, 64, 128, 256, 512
GLA_GATE_RANK, GLA_GATE_NORM, GLA_CHUNK = 16, 16.0, 64
D_FF, CONV_W, N_BRANCH = 2816, 3, 3
EPS = 1e-6

LANE = 128
VMEM_LIMIT = 56 * 1024 * 1024

COL_GATES = 0
COL_SBQ = COL_GATES + N_BRANCH * D_MODEL
COL_SBK = COL_SBQ + SB_WIDTH
COL_SBV = COL_SBK + SB_WIDTH
COL_SGU = COL_SBV + SB_WIDTH
COL_SGV = COL_SGU + SG_WIDTH
COL_GQ = COL_SGV + SG_WIDTH
COL_GK = COL_GQ + GLA_KW
COL_GV = COL_GK + GLA_KW
COL_GR = COL_GV + GLA_VW
COL_GDOWN = COL_GR + GLA_VW
P_WIDTH = COL_GDOWN + LANE

ROW_TILE = 512
PROJ_N_CHUNK = 512
LOG2E = 1.4426950408889634
SB_BLOCK = 256
SB_LAG = 4
SB_UNROLL = 4
FFN_F_CHUNK = 256
FFN_HALO = 16


def _params(n_axes):
    return pltpu.CompilerParams(dimension_semantics=("arbitrary",) * n_axes, vmem_limit_bytes=VMEM_LIMIT)


def _resident(shape):
    zeros = (0,) * len(shape)
    return pl.BlockSpec(shape, lambda *_: zeros, pipeline_mode=pl.Buffered(1))


def _rms(x, g):
    return x * lax.rsqrt(jnp.mean(x * x, axis=-1, keepdims=True) + EPS) * g


def _log_sigmoid_pair(z):
    s = jnp.log(1.0 + jnp.exp(-jnp.abs(z)))
    ls = jnp.minimum(z, 0.0) - s
    return ls, ls - z


def _split_bf16(x):
    hi = x.astype(BF16)
    lo = (x - hi.astype(F32)).astype(BF16)
    return hi, lo


def _dot(a, b):
    return jnp.dot(a, b, preferred_element_type=F32)


def _dot_nt(a, b):
    return lax.dot_general(a, b, (((1,), (1,)), ((), ())), preferred_element_type=F32)


def _dot_tn(a, b):
    return lax.dot_general(a, b, (((0,), (0,)), ((), ())), preferred_element_type=F32)


def _proj_kernel(x_ref, g_ref, w_ref, o_ref):
    xn = _rms(x_ref[...], g_ref[...]).astype(BF16)
    width = o_ref.shape[1]
    for c0 in range(0, width, PROJ_N_CHUNK):
        c1 = min(c0 + PROJ_N_CHUNK, width)
        acc = _dot(xn, w_ref[:, c0:c1])
        if (c0, c1) == (COL_SBQ, COL_SBK):
            acc = acc * LOG2E
        o_ref[:, c0:c1] = acc.astype(o_ref.dtype)


def _proj(h, g, w):
    t = h.shape[0]
    return pl.pallas_call(
        _proj_kernel,
        out_shape=jax.ShapeDtypeStruct((t, P_WIDTH), BF16),
        grid=(t // ROW_TILE,),
        in_specs=[pl.BlockSpec((ROW_TILE, D_MODEL), lambda i: (i, 0)),
                  _resident((1, D_MODEL)),
                  _resident((D_MODEL, P_WIDTH))],
        out_specs=pl.BlockSpec((ROW_TILE, P_WIDTH), lambda i: (i, 0)),
        compiler_params=_params(1),
        name="proj",
    )(h, g, w)


def _sb_kernel(q_ref, k_ref, v_ref, u_ref, o_ref, qs_ref, z_ref, hl_ref, t_ref, r_ref, rs_ref, acc_ref, c_ref):
    s = q_ref.shape[1]
    tq = SB_BLOCK
    nq = s // tq

    lane_s = lax.broadcasted_iota(jnp.int32, (s, LANE), 1)
    q = q_ref[0]
    qs_ref[0] = jnp.where(lane_s < SB_HEAD_DIM, q, jnp.zeros_like(q))
    qs_ref[1] = jnp.where(lane_s < SB_HEAD_DIM, jnp.zeros_like(q), q)
    for ref in (z_ref, hl_ref, t_ref, r_ref, rs_ref, acc_ref, c_ref):
        ref[...] = jnp.zeros_like(ref)

    first_head = lax.broadcasted_iota(jnp.int32, (tq, LANE), 1) < SB_HEAD_DIM
    row = lax.broadcasted_iota(jnp.int32, (2 * tq, tq), 0) & (tq - 1)
    col = lax.broadcasted_iota(jnp.int32, (2 * tq, tq), 1)
    causal = col < row

    def rows(i):
        return pl.ds(pl.multiple_of(i * tq, tq), tq)

    def merge_heads(a):
        return jnp.where(first_head, a[:tq], a[tq:])

    def iteration(it, phase, units, n_units, diag):
        slot = [(phase - k) % SB_LAG for k in range(SB_LAG + 1)]
        i0, j0 = units[0]
        q2 = jnp.concatenate([qs_ref[0, rows(i0), :], qs_ref[1, rows(i0), :]], axis=0)
        z_ref[slot[0]] = _dot_nt(q2, k_ref[0, rows(j0), :])
        t_ref[slot[2]] = _dot(hl_ref[slot[2]], u_ref[...])
        w = jnp.exp2(z_ref[slot[3]] + t_ref[slot[3]])
        if diag:
            w = jnp.where(causal, w, 0.0)
        r_ref[slot[3]] = _dot(w.astype(BF16), v_ref[0, rows(units[3][1]), :])
        z = z_ref[slot[1]]
        nl = jnp.maximum(z, 0.0) + jnp.log2(1.0 + jnp.exp2(-jnp.abs(z)))
        if diag:
            nl = jnp.where(causal, nl, 0.0)
        hi, lo = _split_bf16(nl)
        hl_ref[slot[1], :, :tq] = hi
        hl_ref[slot[1], :, tq:] = lo
        rs_ref[slot[1]] = jnp.sum(nl, axis=1, keepdims=True)
        i4 = units[4][0]
        valid = (it >= SB_LAG) & (it < n_units + SB_LAG)
        contrib = merge_heads(r_ref[slot[4]])
        rsum = merge_heads(jnp.broadcast_to(rs_ref[slot[4]], (2 * tq, LANE)))
        acc, c = acc_ref[rows(i4), :], c_ref[rows(i4), :]
        if diag:
            acc_new, c_new = contrib, -rsum
        else:
            acc_new, c_new = acc + contrib * jnp.exp2(c), c - rsum
        acc_ref[rows(i4), :] = jnp.where(valid, acc_new, acc)
        c_ref[rows(i4), :] = jnp.where(valid, c_new, c)

    def run(n_units, first, advance, diag):
        def body(group, units):
            for phase in range(SB_UNROLL):
                iteration(group * SB_UNROLL + phase, phase, units, n_units, diag)
                units = (advance(*units[0]),) + units[:-1]
            return units

        first = (jnp.int32(first[0]), jnp.int32(first[1]))
        lax.fori_loop(0, pl.cdiv(n_units + SB_LAG, SB_UNROLL), body, (first,) * (SB_LAG + 1))

    def next_diag(i, j):
        i = jnp.minimum(i + 1, nq - 1)
        return i, i

    def next_off(i, j):
        wrap = j == 0
        return jnp.minimum(jnp.where(wrap, i + 1, i), nq - 1), jnp.where(wrap, i, j - 1)

    run(nq, (0, 0), next_diag, True)
    run(nq * (nq - 1) // 2, (1, 0), next_off, False)
    o_ref[0] = acc_ref[...].astype(o_ref.dtype)


def _sb_attention(p3, u):
    b, s, _ = p3.shape
    tq = SB_BLOCK
    qb, kb, vb = COL_SBQ // LANE, COL_SBK // LANE, COL_SBV // LANE
    seq_block = lambda col0: pl.BlockSpec((1, s, LANE), lambda bi, hp: (bi, 0, col0 + hp))
    slots = SB_LAG
    return pl.pallas_call(
        _sb_kernel,
        out_shape=jax.ShapeDtypeStruct((b, s, SB_WIDTH), BF16),
        grid=(b, SB_HEADS // 2),
        in_specs=[seq_block(qb), seq_block(kb), seq_block(vb), _resident((2 * tq, tq))],
        out_specs=pl.BlockSpec((1, s, LANE), lambda bi, hp: (bi, 0, hp)),
        scratch_shapes=[pltpu.VMEM((2, s, LANE), BF16),
                        pltpu.VMEM((slots, 2 * tq, tq), F32),
                        pltpu.VMEM((slots, 2 * tq, 2 * tq), BF16),
                        pltpu.VMEM((slots, 2 * tq, tq), F32),
                        pltpu.VMEM((slots, 2 * tq, LANE), F32),
                        pltpu.VMEM((slots, 2 * tq, 1), F32),
                        pltpu.VMEM((s, LANE), F32),
                        pltpu.VMEM((s, LANE), F32)],
        compiler_params=_params(2),
        name="sb_attention",
    )(p3, p3, p3, u)


def _sg_kernel(u_ref, v_ref, lng_ref, lnb_ref, w_ref, bias_ref, o_ref):
    rows = u_ref.shape[0]
    u = jax.nn.gelu(u_ref[...].astype(F32))
    v = jax.nn.gelu(v_ref[...].astype(F32))
    mu = jnp.mean(v, axis=-1, keepdims=True)
    d = v - mu
    var = jnp.mean(d * d, axis=-1, keepdims=True)
    vn = (d * lax.rsqrt(var + EPS) * lng_ref[...] + lnb_ref[...]).astype(BF16)

    wrow = lax.broadcasted_iota(jnp.int32, (SG_CHUNK, 2 * SG_CHUNK), 0)
    wcol = lax.broadcasted_iota(jnp.int32, (SG_CHUNK, 2 * SG_CHUNK), 1) & (SG_CHUNK - 1)
    tril = wcol <= wrow
    ws = [jnp.where(tril, w_ref[p], jnp.zeros((SG_CHUNK, 2 * SG_CHUNK), BF16)) for p in range(SG_GROUPS // 2)]
    lane = lax.broadcasted_iota(jnp.int32, (SG_CHUNK, LANE), 1)
    first = lane < SG_GROUP_DIM
    zero = jnp.zeros((SG_CHUNK, LANE), BF16)
    bias = bias_ref[...]
    for c in range(rows // SG_CHUNK):
        r0 = c * SG_CHUNK
        vc = vn[r0:r0 + SG_CHUNK]
        cols = []
        for p in range(SG_GROUPS // 2):
            blk = vc[:, p * LANE:(p + 1) * LANE]
            rhs = jnp.concatenate([jnp.where(first, blk, zero), jnp.where(first, zero, blk)], axis=0)
            cols.append(_dot(ws[p], rhs))
        sp = jnp.concatenate(cols, axis=1) + bias
        o_ref[r0:r0 + SG_CHUNK, :] = (u[r0:r0 + SG_CHUNK] * sp).astype(o_ref.dtype)


def _spatial_gating(p2, ln_g, ln_b, w_pairs, bias):
    t = p2.shape[0]
    ub, vb = COL_SGU // SG_WIDTH, COL_SGV // SG_WIDTH
    return pl.pallas_call(
        _sg_kernel,
        out_shape=jax.ShapeDtypeStruct((t, SG_WIDTH), BF16),
        grid=(t // ROW_TILE,),
        in_specs=[pl.BlockSpec((ROW_TILE, SG_WIDTH), lambda i: (i, ub)),
                  pl.BlockSpec((ROW_TILE, SG_WIDTH), lambda i: (i, vb)),
                  _resident((1, SG_WIDTH)),
                  _resident((1, SG_WIDTH)),
                  _resident((SG_GROUPS // 2, SG_CHUNK, 2 * SG_CHUNK)),
                  _resident((SG_CHUNK, SG_WIDTH))],
        out_specs=pl.BlockSpec((ROW_TILE, SG_WIDTH), lambda i: (i, 0)),
        compiler_params=_params(1),
        name="spatial_gating",
    )(p2, p2, ln_g, ln_b, w_pairs, bias)


def _gla_kernel(q_ref, k_ref, v_ref, r_ref, dn_ref, wg_ref, bg_ref, ng_ref, o_ref, state_ref, *, steps_per_seq):
    rows = q_ref.shape[0]

    @pl.when(pl.program_id(0) % steps_per_seq == 0)
    def _():
        state_ref[...] = jnp.zeros_like(state_ref)

    gate = _dot(dn_ref[...], wg_ref[...]) + bg_ref[...]
    log_a = _log_sigmoid_pair(gate)[0] * (1.0 / GLA_GATE_NORM)

    c = GLA_CHUNK
    trow = lax.broadcasted_iota(jnp.int32, (c, c), 0)
    tcol = lax.broadcasted_iota(jnp.int32, (c, c), 1)
    causal = tcol <= trow
    ltri = jnp.where(causal, 1.0, 0.0).astype(BF16)
    klane = lax.broadcasted_iota(jnp.int32, (c, GLA_KW), 1)
    slane = lax.broadcasted_iota(jnp.int32, (GLA_DV, GLA_KW), 1)
    ng = ng_ref[...]

    state = state_ref[...]
    for n in range(rows // c):
        r0 = n * c
        hi, lo = _split_bf16(log_a[r0:r0 + c])
        cum = _dot(ltri, hi) + _dot(ltri, lo)
        cum_last = cum[c - 1:c, :]
        qf = q_ref[r0:r0 + c, :].astype(F32)
        kf = k_ref[r0:r0 + c, :].astype(F32)
        q_dec = (qf * jnp.exp(cum)).astype(BF16)
        k_inv = (kf * jnp.exp(-cum)).astype(BF16)
        k_end = (kf * jnp.exp(cum_last - cum)).astype(BF16)
        state_b = state.astype(BF16)
        upd = jnp.zeros((GLA_DV, GLA_KW), F32)
        for h in range(GLA_HEADS):
            in_head = (klane >= h * GLA_DK) & (klane < (h + 1) * GLA_DK)
            qh = jnp.where(in_head, q_dec, jnp.zeros_like(q_dec))
            attn = jnp.where(causal, _dot_nt(qh, k_inv), 0.0)
            vh = v_ref[r0:r0 + c, h * GLA_DV:(h + 1) * GLA_DV]
            o = _dot(attn.astype(BF16), vh) + _dot_nt(qh, state_b)
            o = _rms(o, ng)
            rh = r_ref[r0:r0 + c, h * GLA_DV:(h + 1) * GLA_DV].astype(F32)
            o_ref[r0:r0 + c, h * GLA_DV:(h + 1) * GLA_DV] = (o * (rh * jax.nn.sigmoid(rh))).astype(o_ref.dtype)
            s_head = (slane >= h * GLA_DK) & (slane < (h + 1) * GLA_DK)
            upd = jnp.where(s_head, _dot_tn(vh, k_end), upd)
        state = state * jnp.exp(cum_last) + upd
    state_ref[...] = state


def _gla(p2, w_gup, b_gate, norm_g, seq):
    t = p2.shape[0]
    kern = functools.partial(_gla_kernel, steps_per_seq=seq // ROW_TILE)
    return pl.pallas_call(
        kern,
        out_shape=jax.ShapeDtypeStruct((t, GLA_VW), BF16),
        grid=(t // ROW_TILE,),
        in_specs=[pl.BlockSpec((ROW_TILE, GLA_KW), lambda i: (i, COL_GQ // GLA_KW)),
                  pl.BlockSpec((ROW_TILE, GLA_KW), lambda i: (i, COL_GK // GLA_KW)),
                  pl.BlockSpec((ROW_TILE, GLA_VW), lambda i: (i, COL_GV // GLA_VW)),
                  pl.BlockSpec((ROW_TILE, GLA_VW), lambda i: (i, COL_GR // GLA_VW)),
                  pl.BlockSpec((ROW_TILE, LANE), lambda i: (i, COL_GDOWN // LANE)),
                  _resident((LANE, GLA_KW)),
                  _resident((1, GLA_KW)),
                  _resident((1, GLA_DV))],
        out_specs=pl.BlockSpec((ROW_TILE, GLA_VW), lambda i: (i, 0)),
        scratch_shapes=[pltpu.VMEM((GLA_DV, GLA_KW), F32)],
        compiler_params=_params(1),
        name="gla",
    )(p2, p2, p2, p2, p2, w_gup, b_gate, norm_g)


def _merge_kernel(ya_ref, yb_ref, yc_ref, g_ref, h_ref, pa_ref, pb_ref, pc_ref, wo_ref, pg_ref, o_ref):
    merged = None
    for n, (y_ref, p_ref) in enumerate(((ya_ref, pa_ref), (yb_ref, pb_ref), (yc_ref, pc_ref))):
        gate = jax.nn.sigmoid(g_ref[:, n * D_MODEL:(n + 1) * D_MODEL].astype(F32))
        term = gate * _dot(y_ref[...], p_ref[...])
        merged = term if merged is None else merged + term
    y = _dot(merged.astype(BF16), wo_ref[...])
    o_ref[...] = h_ref[...] + _rms(y, pg_ref[...])


def _merge(ya, yb, yc, p2, h, p_a, p_b, p_c, w_out, post_g):
    t = h.shape[0]
    branch = pl.BlockSpec((ROW_TILE, SB_WIDTH), lambda i: (i, 0))
    return pl.pallas_call(
        _merge_kernel,
        out_shape=jax.ShapeDtypeStruct((t, D_MODEL), F32),
        grid=(t // ROW_TILE,),
        in_specs=[branch, branch, branch,
                  pl.BlockSpec((ROW_TILE, N_BRANCH * D_MODEL), lambda i: (i, 0)),
                  pl.BlockSpec((ROW_TILE, D_MODEL), lambda i: (i, 0)),
                  _resident((SB_WIDTH, D_MODEL)), _resident((SG_WIDTH, D_MODEL)), _resident((GLA_VW, D_MODEL)),
                  _resident((D_MODEL, D_MODEL)), _resident((1, D_MODEL))],
        out_specs=pl.BlockSpec((ROW_TILE, D_MODEL), lambda i: (i, 0)),
        compiler_params=_params(1),
        name="merge",
    )(ya, yb, yc, p2, h, p_a, p_b, p_c, w_out, post_g)


def _ffn_kernel(halo_ref, h_ref, g1_ref, wup_ref, cw_ref, cb_ref, wdn_ref, g2_ref, o_ref, act_ref, *, seq):
    tm = h_ref.shape[0]
    g1 = g1_ref[...]
    x = h_ref[...]
    xn = _rms(x, g1).astype(BF16)
    seq_start = (pl.program_id(0) * tm) % seq == 0
    xh = _rms(halo_ref[...], g1).astype(BF16)
    xh = jnp.where(seq_start, jnp.zeros_like(xh), xh)
    xe = jnp.concatenate([xh, xn], axis=0)
    fc = FFN_F_CHUNK
    for c in range(D_FF // fc):
        cols = slice(2 * fc * c, 2 * fc * (c + 1))
        hid = _dot(xe, wup_ref[:, cols])
        conv = cb_ref[:, cols]
        for j in range(CONV_W):
            off = FFN_HALO - (CONV_W - 1) + j
            conv = conv + hid[off:off + tm] * cw_ref[j:j + 1, cols]
        act = jax.nn.gelu(conv[:, :fc], approximate=True) * conv[:, fc:]
        act_ref[:, c * fc:(c + 1) * fc] = act.astype(BF16)
    y = _dot(act_ref[...], wdn_ref[...])
    o_ref[...] = x + _rms(y, g2_ref[...])


def _ffn(h, pre_g, w_up, conv_w, conv_b, w_down, post_g, seq):
    t = h.shape[0]
    halo_blocks = ROW_TILE // FFN_HALO
    kern = functools.partial(_ffn_kernel, seq=seq)
    return pl.pallas_call(
        kern,
        out_shape=jax.ShapeDtypeStruct((t, D_MODEL), F32),
        grid=(t // ROW_TILE,),
        in_specs=[pl.BlockSpec((FFN_HALO, D_MODEL), lambda i: (jnp.maximum(i * halo_blocks - 1, 0), 0)),
                  pl.BlockSpec((ROW_TILE, D_MODEL), lambda i: (i, 0)),
                  _resident((1, D_MODEL)),
                  _resident((D_MODEL, 2 * D_FF)),
                  _resident((CONV_W, 2 * D_FF)),
                  _resident((1, 2 * D_FF)),
                  _resident((D_FF, D_MODEL)),
                  _resident((1, D_MODEL))],
        out_specs=pl.BlockSpec((ROW_TILE, D_MODEL), lambda i: (i, 0)),
        scratch_shapes=[pltpu.VMEM((ROW_TILE, D_FF), BF16)],
        compiler_params=_params(1),
        name="ffn",
    )(h, h, pre_g, w_up, conv_w, conv_b, w_down, post_g)


def _interleave_ffn(a):
    lead = a.shape[:-1]
    a = a.reshape(lead + (2, D_FF // FFN_F_CHUNK, FFN_F_CHUNK))
    return jnp.swapaxes(a, -3, -2).reshape(lead + (2 * D_FF,))


def _prep_w_in(w_in):
    sbq, rest1, gq, rest2, gates = jnp.split(
        w_in, (SB_WIDTH, 3 * SB_WIDTH + 2 * SG_WIDTH, 3 * SB_WIDTH + 2 * SG_WIDTH + GLA_KW,
               3 * SB_WIDTH + 2 * SG_WIDTH + 2 * GLA_KW + 2 * GLA_VW + GLA_GATE_RANK), axis=-1)
    pad = jnp.zeros(w_in.shape[:-1] + (LANE - GLA_GATE_RANK,), w_in.dtype)
    return jnp.concatenate([gates, sbq * SB_HEAD_DIM ** -0.5, rest1, gq * GLA_DK ** -0.5, rest2, pad],
                           axis=-1).astype(BF16)


def kernel(x, mix_pre_g, mix_post_g, w_in, sg_ln_g, sg_ln_b, sg_w, sg_b, gla_w_gup, gla_b_gate, gla_norm_g,
           p_a, p_b, p_c, w_out, ffn_pre_g, ffn_post_g, ffn_w_up, ffn_conv_w, ffn_conv_b, ffn_w_down):
    b, s, d = x.shape
    depth = w_in.shape[0]
    assert d == D_MODEL and s % ROW_TILE == 0 and s % SB_BLOCK == 0 and s >= 2 * SB_BLOCK

    w_in_p = _prep_w_in(w_in)
    sg_pairs = sg_w.reshape(depth, SG_GROUPS // 2, 2, SG_CHUNK, SG_CHUNK)
    sg_pairs = jnp.swapaxes(sg_pairs, 2, 3).reshape(depth, SG_GROUPS // 2, SG_CHUNK, 2 * SG_CHUNK).astype(BF16)
    sg_bias = jnp.repeat(jnp.swapaxes(sg_b, 1, 2), SG_GROUP_DIM, axis=2)
    w_gup_p = jnp.pad(gla_w_gup, ((0, 0), (0, LANE - GLA_GATE_RANK), (0, 0))).astype(BF16)
    w_up_r = _interleave_ffn(ffn_w_up).astype(BF16)
    conv_w_r = _interleave_ffn(ffn_conv_w)
    conv_b_r = _interleave_ffn(ffn_conv_b)[:, None, :]
    tri = jnp.arange(SB_BLOCK)
    u_incl = -(tri[:, None] >= tri[None, :]).astype(BF16)
    u_incl = jnp.concatenate([u_incl, u_incl], axis=0)

    row = lambda a, l: a[l][None, :]
    h = x.reshape(b * s, d)
    for l in range(depth):
        p2 = _proj(h, row(mix_pre_g, l), w_in_p[l])
        ya = _sb_attention(p2.reshape(b, s, P_WIDTH), u_incl).reshape(b * s, SB_WIDTH)
        yb = _spatial_gating(p2, row(sg_ln_g, l), row(sg_ln_b, l), sg_pairs[l], sg_bias[l])
        yc = _gla(p2, w_gup_p[l], row(gla_b_gate, l), row(gla_norm_g, l), s)
        h = _merge(ya, yb, yc, p2, h, p_a[l].astype(BF16), p_b[l].astype(BF16), p_c[l].astype(BF16),
                   w_out[l].astype(BF16), row(mix_post_g, l))
        h = _ffn(h, row(ffn_pre_g, l), w_up_r[l], conv_w_r[l], conv_b_r[l], ffn_w_down[l].astype(BF16),
                 row(ffn_post_g, l), s)
    return h.reshape(b, s, d)
```

```python
import functools

import jax
import jax.numpy as jnp
from jax import lax
from jax.experimental import pallas as pl
from jax.experimental.pallas import tpu as pltpu

F32 = jnp.float32
BF16 = jnp.bfloat16

D_MODEL = 1024
SB_HEADS, SB_HEAD_DIM, SB_WIDTH = 8, 64, 512
SG_GROUPS, SG_GROUP_DIM, SG_WIDTH, SG_CHUNK = 8, 64, 512, 128
GLA_HEADS, GLA_DK, GLA_DV, GLA_KW, GLA_VW = 4, 64, 128, 256, 512
GLA_GATE_RANK, GLA_GATE_NORM, GLA_CHUNK = 16, 16.0, 64
D_FF, CONV_W, N_BRANCH = 2816, 3, 3
EPS = 1e-6

LANE = 128
VMEM_LIMIT = 56 * 1024 * 1024

COL_GATES = 0
COL_SBQ = COL_GATES + N_BRANCH * D_MODEL
COL_SBK = COL_SBQ + SB_WIDTH
COL_SBV = COL_SBK + SB_WIDTH
COL_SGU = COL_SBV + SB_WIDTH
COL_SGV = COL_SGU + SG_WIDTH
COL_GQ = COL_SGV + SG_WIDTH
COL_GK = COL_GQ + GLA_KW
COL_GV = COL_GK + GLA_KW
COL_GR = COL_GV + GLA_VW
COL_GDOWN = COL_GR + GLA_VW
P_WIDTH = COL_GDOWN + LANE

ROW_TILE = 512
PROJ_N_CHUNK = 512
LOG2E = 1.4426950408889634
SB_BLOCK = 256
SB_LAG = 4
SB_UNROLL = 4
SB_DEAD_LOG2 = -152.0
FFN_F_CHUNK = 256
FFN_HALO = 16


def _params(n_axes):
    return pltpu.CompilerParams(dimension_semantics=("arbitrary",) * n_axes, vmem_limit_bytes=VMEM_LIMIT)


def _resident(shape):
    zeros = (0,) * len(shape)
    return pl.BlockSpec(shape, lambda *_: zeros, pipeline_mode=pl.Buffered(1))


def _rms(x, g):
    return x * lax.rsqrt(jnp.mean(x * x, axis=-1, keepdims=True) + EPS) * g


def _log_sigmoid_pair(z):
    s = jnp.log(1.0 + jnp.exp(-jnp.abs(z)))
    ls = jnp.minimum(z, 0.0) - s
    return ls, ls - z


def _split_bf16(x):
    hi = x.astype(BF16)
    lo = (x - hi.astype(F32)).astype(BF16)
    return hi, lo


def _dot(a, b):
    return jnp.dot(a, b, preferred_element_type=F32)


def _dot_nt(a, b):
    return lax.dot_general(a, b, (((1,), (1,)), ((), ())), preferred_element_type=F32)


def _dot_tn(a, b):
    return lax.dot_general(a, b, (((0,), (0,)), ((), ())), preferred_element_type=F32)


def _proj_kernel(x_ref, g_ref, w_ref, o_ref):
    xn = _rms(x_ref[...], g_ref[...]).astype(BF16)
    width = o_ref.shape[1]
    for c0 in range(0, width, PROJ_N_CHUNK):
        c1 = min(c0 + PROJ_N_CHUNK, width)
        acc = _dot(xn, w_ref[:, c0:c1])
        if (c0, c1) == (COL_SBQ, COL_SBK):
            acc = acc * LOG2E
        o_ref[:, c0:c1] = acc.astype(o_ref.dtype)


def _proj(h, g, w):
    t = h.shape[0]
    return pl.pallas_call(
        _proj_kernel,
        out_shape=jax.ShapeDtypeStruct((t, P_WIDTH), BF16),
        grid=(t // ROW_TILE,),
        in_specs=[pl.BlockSpec((ROW_TILE, D_MODEL), lambda i: (i, 0)),
                  _resident((1, D_MODEL)),
                  _resident((D_MODEL, P_WIDTH))],
        out_specs=pl.BlockSpec((ROW_TILE, P_WIDTH), lambda i: (i, 0)),
        compiler_params=_params(1),
        name="proj",
    )(h, g, w)


def _sb_kernel(q_ref, k_ref, v_ref, u_ref, o_ref, qs_ref, z_ref, hl_ref, t_ref, r_ref, rs_ref, acc_ref, c_ref):
    s = q_ref.shape[1]
    tq = SB_BLOCK
    nq = s // tq

    lane_s = lax.broadcasted_iota(jnp.int32, (s, LANE), 1)
    q = q_ref[0]
    qs_ref[0] = jnp.where(lane_s < SB_HEAD_DIM, q, jnp.zeros_like(q))
    qs_ref[1] = jnp.where(lane_s < SB_HEAD_DIM, jnp.zeros_like(q), q)
    for ref in (z_ref, hl_ref, t_ref, r_ref, rs_ref, acc_ref, c_ref):
        ref[...] = jnp.zeros_like(ref)

    first_head = lax.broadcasted_iota(jnp.int32, (tq, LANE), 1) < SB_HEAD_DIM
    row = lax.broadcasted_iota(jnp.int32, (2 * tq, tq), 0) & (tq - 1)
    col = lax.broadcasted_iota(jnp.int32, (2 * tq, tq), 1)
    causal = col < row

    def rows(i):
        return pl.ds(pl.multiple_of(i * tq, tq), tq)

    def merge_heads(a):
        return jnp.where(first_head, a[:tq], a[tq:])

    def iteration(it, phase, d, n_units, diag):
        slot = [(phase - k) % SB_LAG for k in range(SB_LAG + 1)]
        units = []
        for k in range(SB_LAG + 1):
            j = jnp.clip(it - k, 0, n_units - 1)
            units.append((d + j, j))
        i0, j0 = units[0]
        q2 = jnp.concatenate([qs_ref[0, rows(i0), :], qs_ref[1, rows(i0), :]], axis=0)
        z_ref[slot[0]] = _dot_nt(q2, k_ref[0, rows(j0), :])
        t_ref[slot[2]] = _dot(hl_ref[slot[2]], u_ref[...])
        w = jnp.exp2(z_ref[slot[3]] + t_ref[slot[3]])
        if diag:
            w = jnp.where(causal, w, 0.0)
        r_ref[slot[3]] = _dot(w.astype(BF16), v_ref[0, rows(units[3][1]), :])
        z = z_ref[slot[1]]
        nl = jnp.maximum(z, 0.0) + jnp.log2(1.0 + jnp.exp2(-jnp.abs(z)))
        if diag:
            nl = jnp.where(causal, nl, 0.0)
        hi, lo = _split_bf16(nl)
        hl_ref[slot[1], :, :tq] = hi
        hl_ref[slot[1], :, tq:] = lo
        rs_ref[slot[1]] = jnp.sum(nl, axis=1, keepdims=True)
        i4 = units[4][0]
        valid = (it >= SB_LAG) & (it < n_units + SB_LAG)
        contrib = merge_heads(r_ref[slot[4]])
        rsum = merge_heads(jnp.broadcast_to(rs_ref[slot[4]], (2 * tq, LANE)))
        acc, c = acc_ref[rows(i4), :], c_ref[rows(i4), :]
        if diag:
            acc_new, c_new = contrib, -rsum
        else:
            acc_new, c_new = acc + contrib * jnp.exp2(c), c - rsum
        acc_ref[rows(i4), :] = jnp.where(valid, acc_new, acc)
        c_ref[rows(i4), :] = jnp.where(valid, c_new, c)

    def wave(d, diag):
        n_units = nq - d

        def body(group, carry):
            for phase in range(SB_UNROLL):
                iteration(group * SB_UNROLL + phase, phase, d, n_units, diag)
            return carry

        lax.fori_loop(0, (n_units + SB_LAG + SB_UNROLL - 1) // SB_UNROLL, body, 0)

    def live_stick(d):
        blk = lax.broadcasted_iota(jnp.int32, (s, LANE), 0) // tq
        return jnp.max(jnp.where(blk >= d, c_ref[...], -jnp.inf))

    wave(0, True)

    def more(carry):
        d, stick = carry
        return (d < nq) & (stick >= SB_DEAD_LOG2)

    def next_wave(carry):
        d, _ = carry
        wave(d, False)
        return d + 1, live_stick(d + 1)

    lax.while_loop(more, next_wave, (jnp.int32(1), live_stick(1)))
    o_ref[0] = acc_ref[...].astype(o_ref.dtype)


def _sb_attention(p3, u):
    b, s, _ = p3.shape
    tq = SB_BLOCK
    qb, kb, vb = COL_SBQ // LANE, COL_SBK // LANE, COL_SBV // LANE
    seq_block = lambda col0: pl.BlockSpec((1, s, LANE), lambda bi, hp: (bi, 0, col0 + hp))
    slots = SB_LAG
    return pl.pallas_call(
        _sb_kernel,
        out_shape=jax.ShapeDtypeStruct((b, s, SB_WIDTH), BF16),
        grid=(b, SB_HEADS // 2),
        in_specs=[seq_block(qb), seq_block(kb), seq_block(vb), _resident((2 * tq, tq))],
        out_specs=pl.BlockSpec((1, s, LANE), lambda bi, hp: (bi, 0, hp)),
        scratch_shapes=[pltpu.VMEM((2, s, LANE), BF16),
                        pltpu.VMEM((slots, 2 * tq, tq), F32),
                        pltpu.VMEM((slots, 2 * tq, 2 * tq), BF16),
                        pltpu.VMEM((slots, 2 * tq, tq), F32),
                        pltpu.VMEM((slots, 2 * tq, LANE), F32),
                        pltpu.VMEM((slots, 2 * tq, 1), F32),
                        pltpu.VMEM((s, LANE), F32),
                        pltpu.VMEM((s, LANE), F32)],
        compiler_params=_params(2),
        name="sb_attention",
    )(p3, p3, p3, u)


def _sg_kernel(u_ref, v_ref, lng_ref, lnb_ref, w_ref, bias_ref, o_ref):
    rows = u_ref.shape[0]
    u = jax.nn.gelu(u_ref[...].astype(F32))
    v = jax.nn.gelu(v_ref[...].astype(F32))
    mu = jnp.mean(v, axis=-1, keepdims=True)
    d = v - mu
    var = jnp.mean(d * d, axis=-1, keepdims=True)
    vn = (d * lax.rsqrt(var + EPS) * lng_ref[...] + lnb_ref[...]).astype(BF16)

    wrow = lax.broadcasted_iota(jnp.int32, (SG_CHUNK, 2 * SG_CHUNK), 0)
    wcol = lax.broadcasted_iota(jnp.int32, (SG_CHUNK, 2 * SG_CHUNK), 1) & (SG_CHUNK - 1)
    tril = wcol <= wrow
    ws = [jnp.where(tril, w_ref[p], jnp.zeros((SG_CHUNK, 2 * SG_CHUNK), BF16)) for p in range(SG_GROUPS // 2)]
    lane = lax.broadcasted_iota(jnp.int32, (SG_CHUNK, LANE), 1)
    first = lane < SG_GROUP_DIM
    zero = jnp.zeros((SG_CHUNK, LANE), BF16)
    bias = bias_ref[...]
    for c in range(rows // SG_CHUNK):
        r0 = c * SG_CHUNK
        vc = vn[r0:r0 + SG_CHUNK]
        cols = []
        for p in range(SG_GROUPS // 2):
            blk = vc[:, p * LANE:(p + 1) * LANE]
            rhs = jnp.concatenate([jnp.where(first, blk, zero), jnp.where(first, zero, blk)], axis=0)
            cols.append(_dot(ws[p], rhs))
        sp = jnp.concatenate(cols, axis=1) + bias
        o_ref[r0:r0 + SG_CHUNK, :] = (u[r0:r0 + SG_CHUNK] * sp).astype(o_ref.dtype)


def _spatial_gating(p2, ln_g, ln_b, w_pairs, bias):
    t = p2.shape[0]
    ub, vb = COL_SGU // SG_WIDTH, COL_SGV // SG_WIDTH
    return pl.pallas_call(
        _sg_kernel,
        out_shape=jax.ShapeDtypeStruct((t, SG_WIDTH), BF16),
        grid=(t // ROW_TILE,),
        in_specs=[pl.BlockSpec((ROW_TILE, SG_WIDTH), lambda i: (i, ub)),
                  pl.BlockSpec((ROW_TILE, SG_WIDTH), lambda i: (i, vb)),
                  _resident((1, SG_WIDTH)),
                  _resident((1, SG_WIDTH)),
                  _resident((SG_GROUPS // 2, SG_CHUNK, 2 * SG_CHUNK)),
                  _resident((SG_CHUNK, SG_WIDTH))],
        out_specs=pl.BlockSpec((ROW_TILE, SG_WIDTH), lambda i: (i, 0)),
        compiler_params=_params(1),
        name="spatial_gating",
    )(p2, p2, ln_g, ln_b, w_pairs, bias)


def _gla_kernel(q_ref, k_ref, v_ref, r_ref, dn_ref, wg_ref, bg_ref, ng_ref, o_ref, state_ref, *, steps_per_seq):
    rows = q_ref.shape[0]

    @pl.when(pl.program_id(0) % steps_per_seq == 0)
    def _():
        state_ref[...] = jnp.zeros_like(state_ref)

    gate = _dot(dn_ref[...], wg_ref[...]) + bg_ref[...]
    log_a = _log_sigmoid_pair(gate)[0] * (1.0 / GLA_GATE_NORM)

    c = GLA_CHUNK
    trow = lax.broadcasted_iota(jnp.int32, (c, c), 0)
    tcol = lax.broadcasted_iota(jnp.int32, (c, c), 1)
    causal = tcol <= trow
    ltri = jnp.where(causal, 1.0, 0.0).astype(BF16)
    klane = lax.broadcasted_iota(jnp.int32, (c, GLA_KW), 1)
    slane = lax.broadcasted_iota(jnp.int32, (GLA_DV, GLA_KW), 1)
    ng = ng_ref[...]

    state = state_ref[...]
    for n in range(rows // c):
        r0 = n * c
        hi, lo = _split_bf16(log_a[r0:r0 + c])
        cum = _dot(ltri, hi) + _dot(ltri, lo)
        cum_last = cum[c - 1:c, :]
        qf = q_ref[r0:r0 + c, :].astype(F32)
        kf = k_ref[r0:r0 + c, :].astype(F32)
        q_dec = (qf * jnp.exp(cum)).astype(BF16)
        k_inv = (kf * jnp.exp(-cum)).astype(BF16)
        k_end = (kf * jnp.exp(cum_last - cum)).astype(BF16)
        state_b = state.astype(BF16)
        upd = jnp.zeros((GLA_DV, GLA_KW), F32)
        for h in range(GLA_HEADS):
            in_head = (klane >= h * GLA_DK) & (klane < (h + 1) * GLA_DK)
            qh = jnp.where(in_head, q_dec, jnp.zeros_like(q_dec))
            attn = jnp.where(causal, _dot_nt(qh, k_inv), 0.0)
            vh = v_ref[r0:r0 + c, h * GLA_DV:(h + 1) * GLA_DV]
            o = _dot(attn.astype(BF16), vh) + _dot_nt(qh, state_b)
            o = _rms(o, ng)
            rh = r_ref[r0:r0 + c, h * GLA_DV:(h + 1) * GLA_DV].astype(F32)
            o_ref[r0:r0 + c, h * GLA_DV:(h + 1) * GLA_DV] = (o * (rh * jax.nn.sigmoid(rh))).astype(o_ref.dtype)
            s_head = (slane >= h * GLA_DK) & (slane < (h + 1) * GLA_DK)
            upd = jnp.where(s_head, _dot_tn(vh, k_end), upd)
        state = state * jnp.exp(cum_last) + upd
    state_ref[...] = state


def _gla(p2, w_gup, b_gate, norm_g, seq):
    t = p2.shape[0]
    kern = functools.partial(_gla_kernel, steps_per_seq=seq // ROW_TILE)
    return pl.pallas_call(
        kern,
        out_shape=jax.ShapeDtypeStruct((t, GLA_VW), BF16),
        grid=(t // ROW_TILE,),
        in_specs=[pl.BlockSpec((ROW_TILE, GLA_KW), lambda i: (i, COL_GQ // GLA_KW)),
                  pl.BlockSpec((ROW_TILE, GLA_KW), lambda i: (i, COL_GK // GLA_KW)),
                  pl.BlockSpec((ROW_TILE, GLA_VW), lambda i: (i, COL_GV // GLA_VW)),
                  pl.BlockSpec((ROW_TILE, GLA_VW), lambda i: (i, COL_GR // GLA_VW)),
                  pl.BlockSpec((ROW_TILE, LANE), lambda i: (i, COL_GDOWN // LANE)),
                  _resident((LANE, GLA_KW)),
                  _resident((1, GLA_KW)),
                  _resident((1, GLA_DV))],
        out_specs=pl.BlockSpec((ROW_TILE, GLA_VW), lambda i: (i, 0)),
        scratch_shapes=[pltpu.VMEM((GLA_DV, GLA_KW), F32)],
        compiler_params=_params(1),
        name="gla",
    )(p2, p2, p2, p2, p2, w_gup, b_gate, norm_g)


def _merge_kernel(ya_ref, yb_ref, yc_ref, g_ref, h_ref, pa_ref, pb_ref, pc_ref, wo_ref, pg_ref, o_ref):
    merged = None
    for n, (y_ref, p_ref) in enumerate(((ya_ref, pa_ref), (yb_ref, pb_ref), (yc_ref, pc_ref))):
        gate = jax.nn.sigmoid(g_ref[:, n * D_MODEL:(n + 1) * D_MODEL].astype(F32))
        term = gate * _dot(y_ref[...], p_ref[...])
        merged = term if merged is None else merged + term
    y = _dot(merged.astype(BF16), wo_ref[...])
    o_ref[...] = h_ref[...] + _rms(y, pg_ref[...])


def _merge(ya, yb, yc, p2, h, p_a, p_b, p_c, w_out, post_g):
    t = h.shape[0]
    branch = pl.BlockSpec((ROW_TILE, SB_WIDTH), lambda i: (i, 0))
    return pl.pallas_call(
        _merge_kernel,
        out_shape=jax.ShapeDtypeStruct((t, D_MODEL), F32),
        grid=(t // ROW_TILE,),
        in_specs=[branch, branch, branch,
                  pl.BlockSpec((ROW_TILE, N_BRANCH * D_MODEL), lambda i: (i, 0)),
                  pl.BlockSpec((ROW_TILE, D_MODEL), lambda i: (i, 0)),
                  _resident((SB_WIDTH, D_MODEL)), _resident((SG_WIDTH, D_MODEL)), _resident((GLA_VW, D_MODEL)),
                  _resident((D_MODEL, D_MODEL)), _resident((1, D_MODEL))],
        out_specs=pl.BlockSpec((ROW_TILE, D_MODEL), lambda i: (i, 0)),
        compiler_params=_params(1),
        name="merge",
    )(ya, yb, yc, p2, h, p_a, p_b, p_c, w_out, post_g)


def _ffn_kernel(halo_ref, h_ref, g1_ref, wup_ref, cw_ref, cb_ref, wdn_ref, g2_ref, o_ref, act_ref, *, seq):
    tm = h_ref.shape[0]
    g1 = g1_ref[...]
    x = h_ref[...]
    xn = _rms(x, g1).astype(BF16)
    seq_start = (pl.program_id(0) * tm) % seq == 0
    xh = _rms(halo_ref[...], g1).astype(BF16)
    xh = jnp.where(seq_start, jnp.zeros_like(xh), xh)
    xe = jnp.concatenate([xh, xn], axis=0)
    fc = FFN_F_CHUNK
    for c in range(D_FF // fc):
        convs = []
        for half in range(2):
            cols = slice(half * D_FF + fc * c, half * D_FF + fc * (c + 1))
            hid = _dot(xe, wup_ref[:, cols])
            conv = cb_ref[:, cols]
            for j in range(CONV_W):
                off = FFN_HALO - (CONV_W - 1) + j
                conv = conv + hid[off:off + tm] * cw_ref[j:j + 1, cols]
            convs.append(conv)
        act = jax.nn.gelu(convs[0], approximate=True) * convs[1]
        act_ref[:, c * fc:(c + 1) * fc] = act.astype(BF16)
    y = _dot(act_ref[...], wdn_ref[...])
    o_ref[...] = x + _rms(y, g2_ref[...])


def _ffn(h, pre_g, w_up, conv_w, conv_b, w_down, post_g, seq):
    t = h.shape[0]
    halo_blocks = ROW_TILE // FFN_HALO
    kern = functools.partial(_ffn_kernel, seq=seq)
    return pl.pallas_call(
        kern,
        out_shape=jax.ShapeDtypeStruct((t, D_MODEL), F32),
        grid=(t // ROW_TILE,),
        in_specs=[pl.BlockSpec((FFN_HALO, D_MODEL), lambda i: (jnp.maximum(i * halo_blocks - 1, 0), 0)),
                  pl.BlockSpec((ROW_TILE, D_MODEL), lambda i: (i, 0)),
                  _resident((1, D_MODEL)),
                  _resident((D_MODEL, 2 * D_FF)),
                  _resident((CONV_W, 2 * D_FF)),
                  _resident((1, 2 * D_FF)),
                  _resident((D_FF, D_MODEL)),
                  _resident((1, D_MODEL))],
        out_specs=pl.BlockSpec((ROW_TILE, D_MODEL), lambda i: (i, 0)),
        scratch_shapes=[pltpu.VMEM((ROW_TILE, D_FF), BF16)],
        compiler_params=_params(1),
        name="ffn",
    )(h, h, pre_g, w_up, conv_w, conv_b, w_down, post_g)


def _prep_w_in(w_in):
    sbq, rest1, gq, rest2, gates = jnp.split(
        w_in, (SB_WIDTH, 3 * SB_WIDTH + 2 * SG_WIDTH, 3 * SB_WIDTH + 2 * SG_WIDTH + GLA_KW,
               3 * SB_WIDTH + 2 * SG_WIDTH + 2 * GLA_KW + 2 * GLA_VW + GLA_GATE_RANK), axis=-1)
    pad = jnp.zeros(w_in.shape[:-1] + (LANE - GLA_GATE_RANK,), w_in.dtype)
    return jnp.concatenate([gates, sbq * SB_HEAD_DIM ** -0.5, rest1, gq * GLA_DK ** -0.5, rest2, pad],
                           axis=-1).astype(BF16)


def kernel(x, mix_pre_g, mix_post_g, w_in, sg_ln_g, sg_ln_b, sg_w, sg_b, gla_w_gup, gla_b_gate, gla_norm_g,
           p_a, p_b, p_c, w_out, ffn_pre_g, ffn_post_g, ffn_w_up, ffn_conv_w, ffn_conv_b, ffn_w_down):
    b, s, d = x.shape
    depth = w_in.shape[0]
    assert d == D_MODEL and s % ROW_TILE == 0 and s % SB_BLOCK == 0 and s >= 2 * SB_BLOCK

    w_in_p = _prep_w_in(w_in)
    sg_pairs = sg_w.reshape(depth, SG_GROUPS // 2, 2, SG_CHUNK, SG_CHUNK)
    sg_pairs = jnp.swapaxes(sg_pairs, 2, 3).reshape(depth, SG_GROUPS // 2, SG_CHUNK, 2 * SG_CHUNK).astype(BF16)
    sg_bias = jnp.repeat(jnp.swapaxes(sg_b, 1, 2), SG_GROUP_DIM, axis=2)
    w_gup_p = jnp.pad(gla_w_gup, ((0, 0), (0, LANE - GLA_GATE_RANK), (0, 0))).astype(BF16)
    w_up_b = ffn_w_up.astype(BF16)
    tri = jnp.arange(SB_BLOCK)
    u_incl = -(tri[:, None] >= tri[None, :]).astype(BF16)
    u_incl = jnp.concatenate([u_incl, u_incl], axis=0)

    row = lambda a, l: a[l][None, :]
    h = x.reshape(b * s, d)
    for l in range(depth):
        p2 = _proj(h, row(mix_pre_g, l), w_in_p[l])
        ya = _sb_attention(p2.reshape(b, s, P_WIDTH), u_incl).reshape(b * s, SB_WIDTH)
        yb = _spatial_gating(p2, row(sg_ln_g, l), row(sg_ln_b, l), sg_pairs[l], sg_bias[l])
        yc = _gla(p2, w_gup_p[l], row(gla_b_gate, l), row(gla_norm_g, l), s)
        h = _merge(ya, yb, yc, p2, h, p_a[l].astype(BF16), p_b[l].astype(BF16), p_c[l].astype(BF16),
                   w_out[l].astype(BF16), row(mix_post_g, l))
        h = _ffn(h, row(ffn_pre_g, l), w_up_b[l], ffn_conv_w[l], row(ffn_conv_b, l), ffn_w_down[l].astype(BF16),
                 row(ffn_post_g, l), s)
    return h.reshape(b, s, d)
```

```python
import functools

import jax
import jax.numpy as jnp
from jax import lax
from jax.experimental import pallas as pl
from jax.experimental.pallas import tpu as pltpu

F32 = jnp.float32
BF16 = jnp.bfloat16

D_MODEL = 1024
SB_HEADS, SB_HEAD_DIM, SB_WIDTH = 8, 64, 512
SG_GROUPS, SG_GROUP_DIM, SG_WIDTH, SG_CHUNK = 8, 64, 512, 128
GLA_HEADS, GLA_DK, GLA_DV, GLA_KW, GLA_VW = 4, 64, 128, 256, 512
GLA_GATE_RANK, GLA_GATE_NORM, GLA_CHUNK = 16, 16.0, 64
D_FF, CONV_W, N_BRANCH = 2816, 3, 3
EPS = 1e-6

LANE = 128
VMEM_LIMIT = 56 * 1024 * 1024

COL_GATES = 0
COL_SBQ = COL_GATES + N_BRANCH * D_MODEL
COL_SBK = COL_SBQ + SB_WIDTH
COL_SBV = COL_SBK + SB_WIDTH
COL_SGU = COL_SBV + SB_WIDTH
COL_SGV = COL_SGU + SG_WIDTH
COL_GQ = COL_SGV + SG_WIDTH
COL_GK = COL_GQ + GLA_KW
COL_GV = COL_GK + GLA_KW
COL_GR = COL_GV + GLA_VW
COL_GDOWN = COL_GR + GLA_VW
P_WIDTH = COL_GDOWN + LANE

ROW_TILE = 512
PROJ_N_CHUNK = 512
LOG2E = 1.4426950408889634
SB_BLOCK = 256
SB_LAG = 4
SB_UNROLL = 4
SB_DEAD_LOG2 = -152.0
FFN_F_CHUNK = 256
FFN_HALO = 16


def _params(n_axes):
    return pltpu.CompilerParams(dimension_semantics=("arbitrary",) * n_axes, vmem_limit_bytes=VMEM_LIMIT)


def _resident(shape):
    zeros = (0,) * len(shape)
    return pl.BlockSpec(shape, lambda *_: zeros, pipeline_mode=pl.Buffered(1))


def _rms(x, g):
    return x * lax.rsqrt(jnp.mean(x * x, axis=-1, keepdims=True) + EPS) * g


def _log_sigmoid_pair(z):
    s = jnp.log(1.0 + jnp.exp(-jnp.abs(z)))
    ls = jnp.minimum(z, 0.0) - s
    return ls, ls - z


def _split_bf16(x):
    hi = x.astype(BF16)
    lo = (x - hi.astype(F32)).astype(BF16)
    return hi, lo


def _dot(a, b):
    return jnp.dot(a, b, preferred_element_type=F32)


def _dot_nt(a, b):
    return lax.dot_general(a, b, (((1,), (1,)), ((), ())), preferred_element_type=F32)


def _dot_tn(a, b):
    return lax.dot_general(a, b, (((0,), (0,)), ((), ())), preferred_element_type=F32)


def _proj_kernel(x_ref, g_ref, w_ref, o_ref):
    xn = _rms(x_ref[...], g_ref[...]).astype(BF16)
    width = o_ref.shape[1]
    for c0 in range(0, width, PROJ_N_CHUNK):
        c1 = min(c0 + PROJ_N_CHUNK, width)
        acc = _dot(xn, w_ref[:, c0:c1])
        if (c0, c1) == (COL_SBQ, COL_SBK):
            acc = acc * LOG2E
        o_ref[:, c0:c1] = acc.astype(o_ref.dtype)


def _proj(h, g, w):
    t = h.shape[0]
    return pl.pallas_call(
        _proj_kernel,
        out_shape=jax.ShapeDtypeStruct((t, P_WIDTH), BF16),
        grid=(t // ROW_TILE,),
        in_specs=[pl.BlockSpec((ROW_TILE, D_MODEL), lambda i: (i, 0)),
                  _resident((1, D_MODEL)),
                  _resident((D_MODEL, P_WIDTH))],
        out_specs=pl.BlockSpec((ROW_TILE, P_WIDTH), lambda i: (i, 0)),
        compiler_params=_params(1),
        name="proj",
    )(h, g, w)


def _sb_kernel(q_ref, k_ref, v_ref, u_ref, o_ref, qs_ref, z_ref, hl_ref, t_ref, r_ref, rs_ref, acc_ref, c_ref):
    s = q_ref.shape[1]
    tq = SB_BLOCK
    nq = s // tq

    lane_s = lax.broadcasted_iota(jnp.int32, (s, LANE), 1)
    q = q_ref[0]
    qs_ref[0] = jnp.where(lane_s < SB_HEAD_DIM, q, jnp.zeros_like(q))
    qs_ref[1] = jnp.where(lane_s < SB_HEAD_DIM, jnp.zeros_like(q), q)
    for ref in (z_ref, hl_ref, t_ref, r_ref, rs_ref, acc_ref, c_ref):
        ref[...] = jnp.zeros_like(ref)

    first_head = lax.broadcasted_iota(jnp.int32, (tq, LANE), 1) < SB_HEAD_DIM
    row = lax.broadcasted_iota(jnp.int32, (2 * tq, tq), 0) & (tq - 1)
    col = lax.broadcasted_iota(jnp.int32, (2 * tq, tq), 1)
    causal = col < row

    def rows(i):
        return pl.ds(pl.multiple_of(i * tq, tq), tq)

    def merge_heads(a):
        return jnp.where(first_head, a[:tq], a[tq:])

    def iteration(it, phase, unit_of, n_units, diag_of):
        slot = [(phase - k) % SB_LAG for k in range(SB_LAG + 1)]
        units = [unit_of(jnp.clip(it - k, 0, n_units - 1)) for k in range(SB_LAG + 1)]
        diag = [diag_of((phase - k) % 2) for k in range(SB_LAG + 1)]
        w = jnp.exp2(z_ref[slot[3]] + t_ref[slot[3]])
        if diag[3]:
            w = jnp.where(causal, w, 0.0)
        r_ref[slot[3]] = _dot(w.astype(BF16), v_ref[0, rows(units[3][1]), :])
        z = z_ref[slot[1]]
        nl = jnp.maximum(z, 0.0) + jnp.log2(1.0 + jnp.exp2(-jnp.abs(z)))
        if diag[1]:
            nl = jnp.where(causal, nl, 0.0)
        hi, lo = _split_bf16(nl)
        hl_ref[slot[1], :, :tq] = hi
        hl_ref[slot[1], :, tq:] = lo
        rs_ref[slot[1]] = jnp.sum(nl, axis=1, keepdims=True)
        t_ref[slot[2]] = _dot(hl_ref[slot[2]], u_ref[...])
        i0, j0, _ = units[0]
        q2 = jnp.concatenate([qs_ref[0, rows(i0), :], qs_ref[1, rows(i0), :]], axis=0)
        z_ref[slot[0]] = _dot_nt(q2, k_ref[0, rows(j0), :])
        i4, _, real = units[4]
        valid = (it >= SB_LAG) & (it < n_units + SB_LAG) & real
        contrib = merge_heads(r_ref[slot[4]])
        rsum = merge_heads(jnp.broadcast_to(rs_ref[slot[4]], (2 * tq, LANE)))
        acc, c = acc_ref[rows(i4), :], c_ref[rows(i4), :]
        if diag[4]:
            acc_new, c_new = contrib, -rsum
        else:
            acc_new, c_new = acc + contrib * jnp.exp2(c), c - rsum
        acc_ref[rows(i4), :] = jnp.where(valid, acc_new, acc)
        c_ref[rows(i4), :] = jnp.where(valid, c_new, c)

    def run(unit_of, n_units, diag_of):
        def body(group, carry):
            for phase in range(SB_UNROLL):
                iteration(group * SB_UNROLL + phase, phase, unit_of, n_units, diag_of)
            return carry

        lax.fori_loop(0, (n_units + SB_LAG + SB_UNROLL - 1) // SB_UNROLL, body, 0)

    def near_unit(u):
        i, odd = u >> 1, u & 1
        return i, jnp.maximum(i - odd, 0), u != 1

    def live_stick(d):
        blk = lax.broadcasted_iota(jnp.int32, (s, LANE), 0) // tq
        return jnp.max(jnp.where(blk >= d, c_ref[...], -jnp.inf))

    run(near_unit, 2 * nq, lambda parity: parity == 0)

    def more(carry):
        d, stick = carry
        return (d < nq) & (stick >= SB_DEAD_LOG2)

    def next_wave(carry):
        d, _ = carry
        run(lambda u: (d + u, u, True), nq - d, lambda parity: False)
        return d + 1, live_stick(d + 1)

    lax.while_loop(more, next_wave, (jnp.int32(2), live_stick(2)))
    o_ref[0] = acc_ref[...].astype(o_ref.dtype)


def _sb_attention(p3, u):
    b, s, _ = p3.shape
    tq = SB_BLOCK
    qb, kb, vb = COL_SBQ // LANE, COL_SBK // LANE, COL_SBV // LANE
    seq_block = lambda col0: pl.BlockSpec((1, s, LANE), lambda bi, hp: (bi, 0, col0 + hp))
    slots = SB_LAG
    return pl.pallas_call(
        _sb_kernel,
        out_shape=jax.ShapeDtypeStruct((b, s, SB_WIDTH), BF16),
        grid=(b, SB_HEADS // 2),
        in_specs=[seq_block(qb), seq_block(kb), seq_block(vb), _resident((2 * tq, tq))],
        out_specs=pl.BlockSpec((1, s, LANE), lambda bi, hp: (bi, 0, hp)),
        scratch_shapes=[pltpu.VMEM((2, s, LANE), BF16),
                        pltpu.VMEM((slots, 2 * tq, tq), F32),
                        pltpu.VMEM((slots, 2 * tq, 2 * tq), BF16),
                        pltpu.VMEM((slots, 2 * tq, tq), F32),
                        pltpu.VMEM((slots, 2 * tq, LANE), F32),
                        pltpu.VMEM((slots, 2 * tq, 1), F32),
                        pltpu.VMEM((s, LANE), F32),
                        pltpu.VMEM((s, LANE), F32)],
        compiler_params=_params(2),
        name="sb_attention",
    )(p3, p3, p3, u)


def _sg_kernel(u_ref, v_ref, lng_ref, lnb_ref, w_ref, bias_ref, o_ref):
    rows = u_ref.shape[0]
    u = jax.nn.gelu(u_ref[...].astype(F32))
    v = jax.nn.gelu(v_ref[...].astype(F32))
    mu = jnp.mean(v, axis=-1, keepdims=True)
    d = v - mu
    var = jnp.mean(d * d, axis=-1, keepdims=True)
    vn = (d * lax.rsqrt(var + EPS) * lng_ref[...] + lnb_ref[...]).astype(BF16)

    wrow = lax.broadcasted_iota(jnp.int32, (SG_CHUNK, 2 * SG_CHUNK), 0)
    wcol = lax.broadcasted_iota(jnp.int32, (SG_CHUNK, 2 * SG_CHUNK), 1) & (SG_CHUNK - 1)
    tril = wcol <= wrow
    ws = [jnp.where(tril, w_ref[p], jnp.zeros((SG_CHUNK, 2 * SG_CHUNK), BF16)) for p in range(SG_GROUPS // 2)]
    lane = lax.broadcasted_iota(jnp.int32, (SG_CHUNK, LANE), 1)
    first = lane < SG_GROUP_DIM
    zero = jnp.zeros((SG_CHUNK, LANE), BF16)
    bias = bias_ref[...]
    for c in range(rows // SG_CHUNK):
        r0 = c * SG_CHUNK
        vc = vn[r0:r0 + SG_CHUNK]
        cols = []
        for p in range(SG_GROUPS // 2):
            blk = vc[:, p * LANE:(p + 1) * LANE]
            rhs = jnp.concatenate([jnp.where(first, blk, zero), jnp.where(first, zero, blk)], axis=0)
            cols.append(_dot(ws[p], rhs))
        sp = jnp.concatenate(cols, axis=1) + bias
        o_ref[r0:r0 + SG_CHUNK, :] = (u[r0:r0 + SG_CHUNK] * sp).astype(o_ref.dtype)


def _spatial_gating(p2, ln_g, ln_b, w_pairs, bias):
    t = p2.shape[0]
    ub, vb = COL_SGU // SG_WIDTH, COL_SGV // SG_WIDTH
    return pl.pallas_call(
        _sg_kernel,
        out_shape=jax.ShapeDtypeStruct((t, SG_WIDTH), BF16),
        grid=(t // ROW_TILE,),
        in_specs=[pl.BlockSpec((ROW_TILE, SG_WIDTH), lambda i: (i, ub)),
                  pl.BlockSpec((ROW_TILE, SG_WIDTH), lambda i: (i, vb)),
                  _resident((1, SG_WIDTH)),
                  _resident((1, SG_WIDTH)),
                  _resident((SG_GROUPS // 2, SG_CHUNK, 2 * SG_CHUNK)),
                  _resident((SG_CHUNK, SG_WIDTH))],
        out_specs=pl.BlockSpec((ROW_TILE, SG_WIDTH), lambda i: (i, 0)),
        compiler_params=_params(1),
        name="spatial_gating",
    )(p2, p2, ln_g, ln_b, w_pairs, bias)


def _gla_kernel(q_ref, k_ref, v_ref, r_ref, dn_ref, wg_ref, bg_ref, ng_ref, lu_ref, o_ref, state_ref, *,
                steps_per_seq):
    rows = q_ref.shape[0]
    c = GLA_CHUNK
    n_chunks = rows // c
    heads = range(GLA_HEADS)

    @pl.when(pl.program_id(0) % steps_per_seq == 0)
    def _():
        state_ref[...] = jnp.zeros_like(state_ref)

    gate = _dot(dn_ref[...], wg_ref[...]) + bg_ref[...]
    log_a = _log_sigmoid_pair(gate)[0] * (1.0 / GLA_GATE_NORM)
    hi, lo = _split_bf16(log_a)
    chunk = lambda a, n: a[n * c:(n + 1) * c]

    lu = lu_ref[...]
    sums = [_dot(lu, jnp.concatenate([chunk(hi, n), chunk(lo, n)], axis=0)) for n in range(n_chunks)]
    cum = jnp.concatenate([x[:c] for x in sums], axis=0)
    rem = jnp.concatenate([x[c:] for x in sums], axis=0)
    qf = q_ref[...].astype(F32)
    kf = k_ref[...].astype(F32)
    q_dec = (qf * jnp.exp(cum)).astype(BF16)
    k_inv = (kf * jnp.exp(-cum)).astype(BF16)
    k_end = (kf * jnp.exp(rem)).astype(BF16)

    klane = lax.broadcasted_iota(jnp.int32, (c, GLA_KW), 1) // GLA_DK
    slane = lax.broadcasted_iota(jnp.int32, (GLA_DV, GLA_KW), 1) // GLA_DK
    causal = (lax.broadcasted_iota(jnp.int32, (GLA_HEADS * c, c), 0) & (c - 1)) >= \
        lax.broadcasted_iota(jnp.int32, (GLA_HEADS * c, c), 1)
    zero_q = jnp.zeros((c, GLA_KW), BF16)

    def state_free_part(n):
        qd = chunk(q_dec, n)
        q4 = jnp.concatenate([jnp.where(klane == h, qd, zero_q) for h in heads], axis=0)
        attn = jnp.where(causal, _dot_nt(q4, chunk(k_inv, n)), 0.0).astype(BF16)
        vn = v_ref[n * c:(n + 1) * c, :]
        o_intra = jnp.concatenate(
            [_dot(attn[h * c:(h + 1) * c], vn[:, h * GLA_DV:(h + 1) * GLA_DV]) for h in heads], axis=0)
        kv = _dot_tn(vn, chunk(k_end, n))
        upd = kv[:GLA_DV]
        for h in heads[1:]:
            upd = jnp.where(slane == h, kv[h * GLA_DV:(h + 1) * GLA_DV], upd)
        return q4, o_intra, upd

    ng = ng_ref[...]
    state = state_ref[...]
    parts = [state_free_part(n) for n in range(n_chunks)]
    for n in range(n_chunks):
        q4, o_intra, upd = parts[n]
        o = o_intra + _dot_nt(q4, state.astype(BF16))
        o = _rms(o, ng)
        rn = r_ref[n * c:(n + 1) * c, :]
        rh = jnp.concatenate([rn[:, h * GLA_DV:(h + 1) * GLA_DV] for h in heads], axis=0).astype(F32)
        y = (o * (rh * jax.nn.sigmoid(rh))).astype(o_ref.dtype)
        for h in heads:
            o_ref[n * c:(n + 1) * c, h * GLA_DV:(h + 1) * GLA_DV] = y[h * c:(h + 1) * c]
        state = state * jnp.exp(cum[(n + 1) * c - 1:(n + 1) * c]) + upd
    state_ref[...] = state


def _gla(p2, w_gup, b_gate, norm_g, lu, seq):
    t = p2.shape[0]
    kern = functools.partial(_gla_kernel, steps_per_seq=seq // ROW_TILE)
    return pl.pallas_call(
        kern,
        out_shape=jax.ShapeDtypeStruct((t, GLA_VW), BF16),
        grid=(t // ROW_TILE,),
        in_specs=[pl.BlockSpec((ROW_TILE, GLA_KW), lambda i: (i, COL_GQ // GLA_KW)),
                  pl.BlockSpec((ROW_TILE, GLA_KW), lambda i: (i, COL_GK // GLA_KW)),
                  pl.BlockSpec((ROW_TILE, GLA_VW), lambda i: (i, COL_GV // GLA_VW)),
                  pl.BlockSpec((ROW_TILE, GLA_VW), lambda i: (i, COL_GR // GLA_VW)),
                  pl.BlockSpec((ROW_TILE, LANE), lambda i: (i, COL_GDOWN // LANE)),
                  _resident((LANE, GLA_KW)),
                  _resident((1, GLA_KW)),
                  _resident((1, GLA_DV)),
                  _resident((2 * GLA_CHUNK, 2 * GLA_CHUNK))],
        out_specs=pl.BlockSpec((ROW_TILE, GLA_VW), lambda i: (i, 0)),
        scratch_shapes=[pltpu.VMEM((GLA_DV, GLA_KW), F32)],
        compiler_params=_params(1),
        name="gla",
    )(p2, p2, p2, p2, p2, w_gup, b_gate, norm_g, lu)


def _merge_kernel(ya_ref, yb_ref, yc_ref, g_ref, h_ref, pa_ref, pb_ref, pc_ref, wo_ref, pg_ref, o_ref):
    merged = None
    for n, (y_ref, p_ref) in enumerate(((ya_ref, pa_ref), (yb_ref, pb_ref), (yc_ref, pc_ref))):
        gate = jax.nn.sigmoid(g_ref[:, n * D_MODEL:(n + 1) * D_MODEL].astype(F32))
        term = gate * _dot(y_ref[...], p_ref[...])
        merged = term if merged is None else merged + term
    y = _dot(merged.astype(BF16), wo_ref[...])
    o_ref[...] = h_ref[...] + _rms(y, pg_ref[...])


def _merge(ya, yb, yc, p2, h, p_a, p_b, p_c, w_out, post_g):
    t = h.shape[0]
    branch = pl.BlockSpec((ROW_TILE, SB_WIDTH), lambda i: (i, 0))
    return pl.pallas_call(
        _merge_kernel,
        out_shape=jax.ShapeDtypeStruct((t, D_MODEL), F32),
        grid=(t // ROW_TILE,),
        in_specs=[branch, branch, branch,
                  pl.BlockSpec((ROW_TILE, N_BRANCH * D_MODEL), lambda i: (i, 0)),
                  pl.BlockSpec((ROW_TILE, D_MODEL), lambda i: (i, 0)),
                  _resident((SB_WIDTH, D_MODEL)), _resident((SG_WIDTH, D_MODEL)), _resident((GLA_VW, D_MODEL)),
                  _resident((D_MODEL, D_MODEL)), _resident((1, D_MODEL))],
        out_specs=pl.BlockSpec((ROW_TILE, D_MODEL), lambda i: (i, 0)),
        compiler_params=_params(1),
        name="merge",
    )(ya, yb, yc, p2, h, p_a, p_b, p_c, w_out, post_g)


def _ffn_kernel(halo_ref, h_ref, g1_ref, wup_ref, cw_ref, cb_ref, wdn_ref, g2_ref, o_ref, act_ref, *, seq):
    tm = h_ref.shape[0]
    g1 = g1_ref[...]
    x = h_ref[...]
    xn = _rms(x, g1).astype(BF16)
    seq_start = (pl.program_id(0) * tm) % seq == 0
    xh = _rms(halo_ref[...], g1).astype(BF16)
    xh = jnp.where(seq_start, jnp.zeros_like(xh), xh)
    xe = jnp.concatenate([xh, xn], axis=0)
    fc = FFN_F_CHUNK
    for c in range(D_FF // fc):
        convs = []
        for half in range(2):
            cols = slice(half * D_FF + fc * c, half * D_FF + fc * (c + 1))
            hid = _dot(xe, wup_ref[:, cols])
            conv = cb_ref[:, cols]
            for j in range(CONV_W):
                off = FFN_HALO - (CONV_W - 1) + j
                conv = conv + hid[off:off + tm] * cw_ref[j:j + 1, cols]
            convs.append(conv)
        act = jax.nn.gelu(convs[0], approximate=True) * convs[1]
        act_ref[:, c * fc:(c + 1) * fc] = act.astype(BF16)
    y = _dot(act_ref[...], wdn_ref[...])
    o_ref[...] = x + _rms(y, g2_ref[...])


def _ffn(h, pre_g, w_up, conv_w, conv_b, w_down, post_g, seq):
    t = h.shape[0]
    halo_blocks = ROW_TILE // FFN_HALO
    kern = functools.partial(_ffn_kernel, seq=seq)
    return pl.pallas_call(
        kern,
        out_shape=jax.ShapeDtypeStruct((t, D_MODEL), F32),
        grid=(t // ROW_TILE,),
        in_specs=[pl.BlockSpec((FFN_HALO, D_MODEL), lambda i: (jnp.maximum(i * halo_blocks - 1, 0), 0)),
                  pl.BlockSpec((ROW_TILE, D_MODEL), lambda i: (i, 0)),
                  _resident((1, D_MODEL)),
                  _resident((D_MODEL, 2 * D_FF)),
                  _resident((CONV_W, 2 * D_FF)),
                  _resident((1, 2 * D_FF)),
                  _resident((D_FF, D_MODEL)),
                  _resident((1, D_MODEL))],
        out_specs=pl.BlockSpec((ROW_TILE, D_MODEL), lambda i: (i, 0)),
        scratch_shapes=[pltpu.VMEM((ROW_TILE, D_FF), BF16)],
        compiler_params=_params(1),
        name="ffn",
    )(h, h, pre_g, w_up, conv_w, conv_b, w_down, post_g)


def _prep_w_in(w_in):
    sbq, rest1, gq, rest2, gates = jnp.split(
        w_in.astype(BF16), (SB_WIDTH, 3 * SB_WIDTH + 2 * SG_WIDTH, 3 * SB_WIDTH + 2 * SG_WIDTH + GLA_KW,
                            3 * SB_WIDTH + 2 * SG_WIDTH + 2 * GLA_KW + 2 * GLA_VW + GLA_GATE_RANK), axis=-1)
    pad = jnp.zeros(w_in.shape[:-1] + (LANE - GLA_GATE_RANK,), BF16)
    return jnp.concatenate([gates, sbq * SB_HEAD_DIM ** -0.5, rest1, gq * GLA_DK ** -0.5, rest2, pad], axis=-1)


def kernel(x, mix_pre_g, mix_post_g, w_in, sg_ln_g, sg_ln_b, sg_w, sg_b, gla_w_gup, gla_b_gate, gla_norm_g,
           p_a, p_b, p_c, w_out, ffn_pre_g, ffn_post_g, ffn_w_up, ffn_conv_w, ffn_conv_b, ffn_w_down):
    b, s, d = x.shape
    depth = w_in.shape[0]
    assert d == D_MODEL and s % ROW_TILE == 0 and s % SB_BLOCK == 0 and s >= 2 * SB_BLOCK

    w_in_p = _prep_w_in(w_in)
    sg_pairs = sg_w.reshape(depth, SG_GROUPS // 2, 2, SG_CHUNK, SG_CHUNK)
    sg_pairs = jnp.swapaxes(sg_pairs, 2, 3).reshape(depth, SG_GROUPS // 2, SG_CHUNK, 2 * SG_CHUNK).astype(BF16)
    sg_bias = jnp.repeat(jnp.swapaxes(sg_b, 1, 2), SG_GROUP_DIM, axis=2)
    w_gup_p = jnp.pad(gla_w_gup, ((0, 0), (0, LANE - GLA_GATE_RANK), (0, 0))).astype(BF16)
    w_up_b = ffn_w_up.astype(BF16)
    tc = jnp.arange(GLA_CHUNK)
    gla_lu = jnp.concatenate([tc[:, None] >= tc[None, :], tc[:, None] < tc[None, :]], axis=0)
    gla_lu = jnp.concatenate([gla_lu, gla_lu], axis=1).astype(BF16)
    tri = jnp.arange(SB_BLOCK)
    u_incl = -(tri[:, None] >= tri[None, :]).astype(BF16)
    u_incl = jnp.concatenate([u_incl, u_incl], axis=0)

    row = lambda a, l: a[l][None, :]
    h = x.reshape(b * s, d)
    for l in range(depth):
        p2 = _proj(h, row(mix_pre_g, l), w_in_p[l])
        ya = _sb_attention(p2.reshape(b, s, P_WIDTH), u_incl).reshape(b * s, SB_WIDTH)
        yb = _spatial_gating(p2, row(sg_ln_g, l), row(sg_ln_b, l), sg_pairs[l], sg_bias[l])
        yc = _gla(p2, w_gup_p[l], row(gla_b_gate, l), row(gla_norm_g, l), gla_lu, s)
        h = _merge(ya, yb, yc, p2, h, p_a[l].astype(BF16), p_b[l].astype(BF16), p_c[l].astype(BF16),
                   w_out[l].astype(BF16), row(mix_post_g, l))
        h = _ffn(h, row(ffn_pre_g, l), w_up_b[l], ffn_conv_w[l], row(ffn_conv_b, l), ffn_w_down[l].astype(BF16),
                 row(ffn_post_g, l), s)
    return h.reshape(b, s, d)
```

```python
import functools

import jax
import jax.numpy as jnp
from jax import lax
from jax.experimental import pallas as pl
from jax.experimental.pallas import tpu as pltpu

F32 = jnp.float32
BF16 = jnp.bfloat16

D_MODEL = 1024
SB_HEADS, SB_HEAD_DIM, SB_WIDTH = 8, 64, 512
SG_GROUPS, SG_GROUP_DIM, SG_WIDTH, SG_CHUNK = 8, 64, 512, 128
GLA_HEADS, GLA_DK, GLA_DV, GLA_KW, GLA_VW = 4, 64, 128, 256, 512
GLA_GATE_RANK, GLA_GATE_NORM, GLA_CHUNK = 16, 16.0, 64
D_FF, CONV_W, N_BRANCH = 2816, 3, 3
EPS = 1e-6

LANE = 128
VMEM_LIMIT = 56 * 1024 * 1024

COL_GATES = 0
COL_SBQ = COL_GATES + N_BRANCH * D_MODEL
COL_SBK = COL_SBQ + SB_WIDTH
COL_SBV = COL_SBK + SB_WIDTH
COL_SGU = COL_SBV + SB_WIDTH
COL_SGV = COL_SGU + SG_WIDTH
COL_GQ = COL_SGV + SG_WIDTH
COL_GK = COL_GQ + GLA_KW
COL_GV = COL_GK + GLA_KW
COL_GR = COL_GV + GLA_VW
COL_GDOWN = COL_GR + GLA_VW
P_WIDTH = COL_GDOWN + LANE

ROW_TILE = 512
FFN_ROW_TILE = 1024
PROJ_N_CHUNK = 512
LOG2E = 1.4426950408889634
SB_BLOCK = 256
SB_LAG = 4
SB_UNROLL = 4
SB_DEAD_LOG2 = -152.0
FFN_F_CHUNK = 256
FFN_HALO = 16


def _params(n_axes):
    return pltpu.CompilerParams(dimension_semantics=("arbitrary",) * n_axes, vmem_limit_bytes=VMEM_LIMIT)


def _resident(shape):
    zeros = (0,) * len(shape)
    return pl.BlockSpec(shape, lambda *_: zeros, pipeline_mode=pl.Buffered(1))


def _rms(x, g):
    return x * lax.rsqrt(jnp.mean(x * x, axis=-1, keepdims=True) + EPS) * g


def _log_sigmoid_pair(z):
    s = jnp.log(1.0 + jnp.exp(-jnp.abs(z)))
    ls = jnp.minimum(z, 0.0) - s
    return ls, ls - z


def _split_bf16(x):
    hi = x.astype(BF16)
    lo = (x - hi.astype(F32)).astype(BF16)
    return hi, lo


def _dot(a, b):
    return jnp.dot(a, b, preferred_element_type=F32)


def _dot_nt(a, b):
    return lax.dot_general(a, b, (((1,), (1,)), ((), ())), preferred_element_type=F32)


def _dot_tn(a, b):
    return lax.dot_general(a, b, (((0,), (0,)), ((), ())), preferred_element_type=F32)


def _proj_kernel(x_ref, g_ref, w_ref, o_ref):
    xn = _rms(x_ref[...], g_ref[...]).astype(BF16)
    width = o_ref.shape[1]
    for c0 in range(0, width, PROJ_N_CHUNK):
        c1 = min(c0 + PROJ_N_CHUNK, width)
        acc = _dot(xn, w_ref[:, c0:c1])
        if (c0, c1) == (COL_SBQ, COL_SBK):
            acc = acc * LOG2E
        o_ref[:, c0:c1] = acc.astype(o_ref.dtype)


def _proj(h, g, w):
    t = h.shape[0]
    return pl.pallas_call(
        _proj_kernel,
        out_shape=jax.ShapeDtypeStruct((t, P_WIDTH), BF16),
        grid=(t // ROW_TILE,),
        in_specs=[pl.BlockSpec((ROW_TILE, D_MODEL), lambda i: (i, 0)),
                  _resident((1, D_MODEL)),
                  _resident((D_MODEL, P_WIDTH))],
        out_specs=pl.BlockSpec((ROW_TILE, P_WIDTH), lambda i: (i, 0)),
        compiler_params=_params(1),
        name="proj",
    )(h, g, w)


def _sb_kernel(q_ref, k_ref, v_ref, u_ref, o_ref, qs_ref, z_ref, hl_ref, t_ref, r_ref, rs_ref, acc_ref, c_ref):
    s = q_ref.shape[1]
    tq = SB_BLOCK
    nq = s // tq

    lane_s = lax.broadcasted_iota(jnp.int32, (s, LANE), 1)
    q = q_ref[0]
    qs_ref[0] = jnp.where(lane_s < SB_HEAD_DIM, q, jnp.zeros_like(q))
    qs_ref[1] = jnp.where(lane_s < SB_HEAD_DIM, jnp.zeros_like(q), q)
    @pl.when((pl.program_id(0) == 0) & (pl.program_id(1) == 0))
    def _():
        for ref in (z_ref, hl_ref, t_ref, r_ref, rs_ref, acc_ref, c_ref):
            ref[...] = jnp.zeros_like(ref)

    first_head = lax.broadcasted_iota(jnp.int32, (tq, LANE), 1) < SB_HEAD_DIM
    row = lax.broadcasted_iota(jnp.int32, (2 * tq, tq), 0) & (tq - 1)
    col = lax.broadcasted_iota(jnp.int32, (2 * tq, tq), 1)
    causal = col < row

    def rows(i):
        return pl.ds(pl.multiple_of(i * tq, tq), tq)

    def merge_heads(a):
        return jnp.where(first_head, a[:tq], a[tq:])

    def iteration(it, phase, unit_of, n_units, diag_of):
        slot = [(phase - k) % SB_LAG for k in range(SB_LAG + 1)]
        units = [unit_of(jnp.clip(it - k, 0, n_units - 1)) for k in range(SB_LAG + 1)]
        diag = [diag_of((phase - k) % 2) for k in range(SB_LAG + 1)]
        w = jnp.exp2(z_ref[slot[3]] + t_ref[slot[3]])
        if diag[3]:
            w = jnp.where(causal, w, 0.0)
        r_ref[slot[3]] = _dot(w.astype(BF16), v_ref[0, rows(units[3][1]), :])
        z = z_ref[slot[1]]
        nl = jnp.maximum(z, 0.0) + jnp.log2(1.0 + jnp.exp2(-jnp.abs(z)))
        if diag[1]:
            nl = jnp.where(causal, nl, 0.0)
        hi, lo = _split_bf16(nl)
        hl_ref[slot[1], :, :tq] = hi
        hl_ref[slot[1], :, tq:] = lo
        rs_ref[slot[1]] = jnp.sum(nl, axis=1, keepdims=True)
        t_ref[slot[2]] = _dot(hl_ref[slot[2]], u_ref[...])
        i0, j0, _ = units[0]
        q2 = jnp.concatenate([qs_ref[0, rows(i0), :], qs_ref[1, rows(i0), :]], axis=0)
        z_ref[slot[0]] = _dot_nt(q2, k_ref[0, rows(j0), :])
        i4, _, real = units[4]
        valid = (it >= SB_LAG) & (it < n_units + SB_LAG) & real
        contrib = merge_heads(r_ref[slot[4]])
        rsum = merge_heads(jnp.broadcast_to(rs_ref[slot[4]], (2 * tq, LANE)))
        acc, c = acc_ref[rows(i4), :], c_ref[rows(i4), :]
        if diag[4]:
            acc_new, c_new = contrib, -rsum
        else:
            acc_new, c_new = acc + contrib * jnp.exp2(c), c - rsum
        acc_ref[rows(i4), :] = jnp.where(valid, acc_new, acc)
        c_ref[rows(i4), :] = jnp.where(valid, c_new, c)

    def run(unit_of, n_units, diag_of):
        def body(group, carry):
            for phase in range(SB_UNROLL):
                iteration(group * SB_UNROLL + phase, phase, unit_of, n_units, diag_of)
            return carry

        lax.fori_loop(0, (n_units + SB_LAG + SB_UNROLL - 1) // SB_UNROLL, body, 0)

    def near_unit(u):
        i, odd = u >> 1, u & 1
        return i, jnp.maximum(i - odd, 0), u != 1

    def live_stick(d):
        blk = lax.broadcasted_iota(jnp.int32, (s, LANE), 0) // tq
        return jnp.max(jnp.where(blk >= d, c_ref[...], -jnp.inf))

    run(near_unit, 2 * nq, lambda parity: parity == 0)

    def more(carry):
        d, stick = carry
        return (d < nq) & (stick >= SB_DEAD_LOG2)

    def next_wave(carry):
        d, _ = carry
        run(lambda u: (d + u, u, True), nq - d, lambda parity: False)
        return d + 1, live_stick(d + 1)

    lax.while_loop(more, next_wave, (jnp.int32(2), live_stick(2)))
    o_ref[0] = acc_ref[...].astype(o_ref.dtype)


def _sb_attention(p3, u):
    b, s, _ = p3.shape
    tq = SB_BLOCK
    qb, kb, vb = COL_SBQ // LANE, COL_SBK // LANE, COL_SBV // LANE
    seq_block = lambda col0: pl.BlockSpec((1, s, LANE), lambda bi, hp: (bi, 0, col0 + hp))
    slots = SB_LAG
    return pl.pallas_call(
        _sb_kernel,
        out_shape=jax.ShapeDtypeStruct((b, s, SB_WIDTH), BF16),
        grid=(b, SB_HEADS // 2),
        in_specs=[seq_block(qb), seq_block(kb), seq_block(vb), _resident((2 * tq, tq))],
        out_specs=pl.BlockSpec((1, s, LANE), lambda bi, hp: (bi, 0, hp)),
        scratch_shapes=[pltpu.VMEM((2, s, LANE), BF16),
                        pltpu.VMEM((slots, 2 * tq, tq), F32),
                        pltpu.VMEM((slots, 2 * tq, 2 * tq), BF16),
                        pltpu.VMEM((slots, 2 * tq, tq), F32),
                        pltpu.VMEM((slots, 2 * tq, LANE), F32),
                        pltpu.VMEM((slots, 2 * tq, 1), F32),
                        pltpu.VMEM((s, LANE), F32),
                        pltpu.VMEM((s, LANE), F32)],
        compiler_params=_params(2),
        name="sb_attention",
    )(p3, p3, p3, u)


def _sg_branch(u_ref, v_ref, lng_ref, lnb_ref, w_ref, bias_ref):
    rows = u_ref.shape[0]
    u = jax.nn.gelu(u_ref[...].astype(F32))
    v = jax.nn.gelu(v_ref[...].astype(F32))
    mu = jnp.mean(v, axis=-1, keepdims=True)
    d = v - mu
    var = jnp.mean(d * d, axis=-1, keepdims=True)
    vn = (d * lax.rsqrt(var + EPS) * lng_ref[...] + lnb_ref[...]).astype(BF16)

    wrow = lax.broadcasted_iota(jnp.int32, (SG_CHUNK, 2 * SG_CHUNK), 0)
    wcol = lax.broadcasted_iota(jnp.int32, (SG_CHUNK, 2 * SG_CHUNK), 1) & (SG_CHUNK - 1)
    tril = wcol <= wrow
    ws = [jnp.where(tril, w_ref[p], jnp.zeros((SG_CHUNK, 2 * SG_CHUNK), BF16)) for p in range(SG_GROUPS // 2)]
    lane = lax.broadcasted_iota(jnp.int32, (SG_CHUNK, LANE), 1)
    first = lane < SG_GROUP_DIM
    zero = jnp.zeros((SG_CHUNK, LANE), BF16)
    bias = bias_ref[...]
    out = []
    for c in range(rows // SG_CHUNK):
        r0 = c * SG_CHUNK
        vc = vn[r0:r0 + SG_CHUNK]
        cols = []
        for p in range(SG_GROUPS // 2):
            blk = vc[:, p * LANE:(p + 1) * LANE]
            rhs = jnp.concatenate([jnp.where(first, blk, zero), jnp.where(first, zero, blk)], axis=0)
            cols.append(_dot(ws[p], rhs))
        sp = jnp.concatenate(cols, axis=1) + bias
        out.append((u[r0:r0 + SG_CHUNK] * sp).astype(BF16))
    return jnp.concatenate(out, axis=0)


def _gla_branch(q_ref, k_ref, v_ref, r_ref, dn_ref, wg_ref, bg_ref, ng_ref, lu_ref, state_ref):
    rows = q_ref.shape[0]
    c = GLA_CHUNK
    n_chunks = rows // c
    heads = range(GLA_HEADS)

    gate = _dot(dn_ref[...], wg_ref[...]) + bg_ref[...]
    log_a = _log_sigmoid_pair(gate)[0] * (1.0 / GLA_GATE_NORM)
    hi, lo = _split_bf16(log_a)
    chunk = lambda a, n: a[n * c:(n + 1) * c]

    lu = lu_ref[...]
    sums = [_dot(lu, jnp.concatenate([chunk(hi, n), chunk(lo, n)], axis=0)) for n in range(n_chunks)]
    cum = jnp.concatenate([x[:c] for x in sums], axis=0)
    rem = jnp.concatenate([x[c:] for x in sums], axis=0)
    qf = q_ref[...].astype(F32)
    kf = k_ref[...].astype(F32)
    q_dec = (qf * jnp.exp(cum)).astype(BF16)
    k_inv = (kf * jnp.exp(-cum)).astype(BF16)
    k_end = (kf * jnp.exp(rem)).astype(BF16)

    klane = lax.broadcasted_iota(jnp.int32, (c, GLA_KW), 1) // GLA_DK
    slane = lax.broadcasted_iota(jnp.int32, (GLA_DV, GLA_KW), 1) // GLA_DK
    causal = (lax.broadcasted_iota(jnp.int32, (GLA_HEADS * c, c), 0) & (c - 1)) >= \
        lax.broadcasted_iota(jnp.int32, (GLA_HEADS * c, c), 1)
    zero_q = jnp.zeros((c, GLA_KW), BF16)

    def state_free_part(n):
        qd = chunk(q_dec, n)
        q4 = jnp.concatenate([jnp.where(klane == h, qd, zero_q) for h in heads], axis=0)
        attn = jnp.where(causal, _dot_nt(q4, chunk(k_inv, n)), 0.0).astype(BF16)
        vn = v_ref[n * c:(n + 1) * c, :]
        o_intra = jnp.concatenate(
            [_dot(attn[h * c:(h + 1) * c], vn[:, h * GLA_DV:(h + 1) * GLA_DV]) for h in heads], axis=0)
        kv = _dot_tn(vn, chunk(k_end, n))
        upd = kv[:GLA_DV]
        for h in heads[1:]:
            upd = jnp.where(slane == h, kv[h * GLA_DV:(h + 1) * GLA_DV], upd)
        return q4, o_intra, upd

    ng = ng_ref[...]
    state = state_ref[...]
    parts = [state_free_part(n) for n in range(n_chunks)]
    out = []
    for n in range(n_chunks):
        q4, o_intra, upd = parts[n]
        o = o_intra + _dot_nt(q4, state.astype(BF16))
        o = _rms(o, ng)
        rn = r_ref[n * c:(n + 1) * c, :]
        rh = jnp.concatenate([rn[:, h * GLA_DV:(h + 1) * GLA_DV] for h in heads], axis=0).astype(F32)
        y = (o * (rh * jax.nn.sigmoid(rh))).astype(BF16)
        out.append(jnp.concatenate([y[h * c:(h + 1) * c] for h in heads], axis=1))
        state = state * jnp.exp(cum[(n + 1) * c - 1:(n + 1) * c]) + upd
    state_ref[...] = state
    return jnp.concatenate(out, axis=0)


def _mix_kernel(ya_ref, g_ref, h_ref,
                sgu_ref, sgv_ref, lng_ref, lnb_ref, sgw_ref, sgb_ref,
                gq_ref, gk_ref, gv_ref, gr_ref, gdn_ref, wg_ref, bg_ref, ng_ref, lu_ref,
                pa_ref, pb_ref, pc_ref, wo_ref, pg_ref, o_ref, state_ref, *, steps_per_seq):
    @pl.when(pl.program_id(0) % steps_per_seq == 0)
    def _():
        state_ref[...] = jnp.zeros_like(state_ref)

    gate = lambda n: jax.nn.sigmoid(g_ref[:, n * D_MODEL:(n + 1) * D_MODEL].astype(F32))
    merged = gate(0) * _dot(ya_ref[...], pa_ref[...])
    yb = _sg_branch(sgu_ref, sgv_ref, lng_ref, lnb_ref, sgw_ref, sgb_ref)
    merged = merged + gate(1) * _dot(yb, pb_ref[...])
    yc = _gla_branch(gq_ref, gk_ref, gv_ref, gr_ref, gdn_ref, wg_ref, bg_ref, ng_ref, lu_ref, state_ref)
    merged = merged + gate(2) * _dot(yc, pc_ref[...])
    y = _dot(merged.astype(BF16), wo_ref[...])
    o_ref[...] = h_ref[...] + _rms(y, pg_ref[...])


def _mix(ya, p2, h, sg_ln_g, sg_ln_b, sg_pairs, sg_bias, w_gup, b_gate, norm_g, lu, p_a, p_b, p_c, w_out, post_g,
         seq):
    t = h.shape[0]
    tile = lambda width, col0: pl.BlockSpec((ROW_TILE, width), lambda i: (i, col0 // width))
    kern = functools.partial(_mix_kernel, steps_per_seq=seq // ROW_TILE)
    return pl.pallas_call(
        kern,
        out_shape=jax.ShapeDtypeStruct((t, D_MODEL), F32),
        grid=(t // ROW_TILE,),
        in_specs=[tile(SB_WIDTH, 0), tile(N_BRANCH * D_MODEL, COL_GATES), tile(D_MODEL, 0),
                  tile(SG_WIDTH, COL_SGU), tile(SG_WIDTH, COL_SGV),
                  _resident((1, SG_WIDTH)), _resident((1, SG_WIDTH)),
                  _resident((SG_GROUPS // 2, SG_CHUNK, 2 * SG_CHUNK)), _resident((SG_CHUNK, SG_WIDTH)),
                  tile(GLA_KW, COL_GQ), tile(GLA_KW, COL_GK), tile(GLA_VW, COL_GV), tile(GLA_VW, COL_GR),
                  tile(LANE, COL_GDOWN),
                  _resident((LANE, GLA_KW)), _resident((1, GLA_KW)), _resident((1, GLA_DV)),
                  _resident((2 * GLA_CHUNK, 2 * GLA_CHUNK)),
                  _resident((SB_WIDTH, D_MODEL)), _resident((SG_WIDTH, D_MODEL)), _resident((GLA_VW, D_MODEL)),
                  _resident((D_MODEL, D_MODEL)), _resident((1, D_MODEL))],
        out_specs=pl.BlockSpec((ROW_TILE, D_MODEL), lambda i: (i, 0)),
        scratch_shapes=[pltpu.VMEM((GLA_DV, GLA_KW), F32)],
        compiler_params=_params(1),
        name="mix",
    )(ya, p2, h, p2, p2, sg_ln_g, sg_ln_b, sg_pairs, sg_bias, p2, p2, p2, p2, p2, w_gup, b_gate, norm_g, lu,
      p_a, p_b, p_c, w_out, post_g)


def _ffn_kernel(halo_ref, h_ref, g1_ref, wup_ref, cw_ref, cb_ref, wdn_ref, g2_ref, o_ref, act_ref, *, seq):
    tm = h_ref.shape[0]
    g1 = g1_ref[...]
    x = h_ref[...]
    xn = _rms(x, g1).astype(BF16)
    seq_start = (pl.program_id(0) * tm) % seq == 0
    xh = _rms(halo_ref[...], g1).astype(BF16)
    xh = jnp.where(seq_start, jnp.zeros_like(xh), xh)
    xe = jnp.concatenate([xh, xn], axis=0)
    fc = FFN_F_CHUNK
    for c in range(D_FF // fc):
        convs = []
        for half in range(2):
            cols = slice(half * D_FF + fc * c, half * D_FF + fc * (c + 1))
            hid = _dot(xe, wup_ref[:, cols])
            conv = cb_ref[:, cols]
            for j in range(CONV_W):
                off = FFN_HALO - (CONV_W - 1) + j
                conv = conv + hid[off:off + tm] * cw_ref[j:j + 1, cols]
            convs.append(conv)
        act = jax.nn.gelu(convs[0], approximate=True) * convs[1]
        act_ref[:, c * fc:(c + 1) * fc] = act.astype(BF16)
    y = _dot(act_ref[...], wdn_ref[...])
    o_ref[...] = x + _rms(y, g2_ref[...])


def _ffn(h, pre_g, w_up, conv_w, conv_b, w_down, post_g, seq):
    t = h.shape[0]
    halo_blocks = FFN_ROW_TILE // FFN_HALO
    kern = functools.partial(_ffn_kernel, seq=seq)
    return pl.pallas_call(
        kern,
        out_shape=jax.ShapeDtypeStruct((t, D_MODEL), F32),
        grid=(t // FFN_ROW_TILE,),
        in_specs=[pl.BlockSpec((FFN_HALO, D_MODEL), lambda i: (jnp.maximum(i * halo_blocks - 1, 0), 0)),
                  pl.BlockSpec((FFN_ROW_TILE, D_MODEL), lambda i: (i, 0)),
                  _resident((1, D_MODEL)),
                  _resident((D_MODEL, 2 * D_FF)),
                  _resident((CONV_W, 2 * D_FF)),
                  _resident((1, 2 * D_FF)),
                  _resident((D_FF, D_MODEL)),
                  _resident((1, D_MODEL))],
        out_specs=pl.BlockSpec((FFN_ROW_TILE, D_MODEL), lambda i: (i, 0)),
        scratch_shapes=[pltpu.VMEM((FFN_ROW_TILE, D_FF), BF16)],
        compiler_params=_params(1),
        name="ffn",
    )(h, h, pre_g, w_up, conv_w, conv_b, w_down, post_g)


def _prep_w_in(w_in):
    sbq, rest1, gq, rest2, gates = jnp.split(
        w_in.astype(BF16), (SB_WIDTH, 3 * SB_WIDTH + 2 * SG_WIDTH, 3 * SB_WIDTH + 2 * SG_WIDTH + GLA_KW,
                            3 * SB_WIDTH + 2 * SG_WIDTH + 2 * GLA_KW + 2 * GLA_VW + GLA_GATE_RANK), axis=-1)
    pad = jnp.zeros(w_in.shape[:-1] + (LANE - GLA_GATE_RANK,), BF16)
    return jnp.concatenate([gates, sbq * SB_HEAD_DIM ** -0.5, rest1, gq * GLA_DK ** -0.5, rest2, pad], axis=-1)


def kernel(x, mix_pre_g, mix_post_g, w_in, sg_ln_g, sg_ln_b, sg_w, sg_b, gla_w_gup, gla_b_gate, gla_norm_g,
           p_a, p_b, p_c, w_out, ffn_pre_g, ffn_post_g, ffn_w_up, ffn_conv_w, ffn_conv_b, ffn_w_down):
    b, s, d = x.shape
    depth = w_in.shape[0]
    assert d == D_MODEL and s % ROW_TILE == 0 and s % FFN_ROW_TILE == 0 and s % SB_BLOCK == 0 and s >= 2 * SB_BLOCK

    w_in_p = _prep_w_in(w_in)
    sg_pairs = sg_w.reshape(depth, SG_GROUPS // 2, 2, SG_CHUNK, SG_CHUNK)
    sg_pairs = jnp.swapaxes(sg_pairs, 2, 3).reshape(depth, SG_GROUPS // 2, SG_CHUNK, 2 * SG_CHUNK).astype(BF16)
    sg_bias = jnp.repeat(jnp.swapaxes(sg_b, 1, 2), SG_GROUP_DIM, axis=2)
    w_gup_p = jnp.pad(gla_w_gup, ((0, 0), (0, LANE - GLA_GATE_RANK), (0, 0))).astype(BF16)
    w_up_b = ffn_w_up.astype(BF16)
    tc = jnp.arange(GLA_CHUNK)
    gla_lu = jnp.concatenate([tc[:, None] >= tc[None, :], tc[:, None] < tc[None, :]], axis=0)
    gla_lu = jnp.concatenate([gla_lu, gla_lu], axis=1).astype(BF16)
    tri = jnp.arange(SB_BLOCK)
    u_incl = -(tri[:, None] >= tri[None, :]).astype(BF16)
    u_incl = jnp.concatenate([u_incl, u_incl], axis=0)

    row = lambda a, l: a[l][None, :]
    h = x.reshape(b * s, d)
    for l in range(depth):
        p2 = _proj(h, row(mix_pre_g, l), w_in_p[l])
        ya = _sb_attention(p2.reshape(b, s, P_WIDTH), u_incl).reshape(b * s, SB_WIDTH)
        h = _mix(ya, p2, h, row(sg_ln_g, l), row(sg_ln_b, l), sg_pairs[l], sg_bias[l],
                 w_gup_p[l], row(gla_b_gate, l), row(gla_norm_g, l), gla_lu,
                 p_a[l].astype(BF16), p_b[l].astype(BF16), p_c[l].astype(BF16), w_out[l].astype(BF16),
                 row(mix_post_g, l), s)
        h = _ffn(h, row(ffn_pre_g, l), w_up_b[l], ffn_conv_w[l], row(ffn_conv_b, l), ffn_w_down[l].astype(BF16),
                 row(ffn_post_g, l), s)
    return h.reshape(b, s, d)
```

```python
import functools

import jax
import jax.numpy as jnp
from jax import lax
from jax.experimental import pallas as pl
from jax.experimental.pallas import tpu as pltpu

F32 = jnp.float32
BF16 = jnp.bfloat16

D_MODEL = 1024
SB_HEADS, SB_HEAD_DIM, SB_WIDTH = 8, 64, 512
SG_GROUPS, SG_GROUP_DIM, SG_WIDTH, SG_CHUNK = 8, 64, 512, 128
GLA_HEADS, GLA_DK, GLA_DV, GLA_KW, GLA_VW = 4, 64, 128, 256, 512
GLA_GATE_RANK, GLA_GATE_NORM, GLA_CHUNK = 16, 16.0, 64
D_FF, CONV_W, N_BRANCH = 2816, 3, 3
EPS = 1e-6

LANE = 128
VMEM_LIMIT = 56 * 1024 * 1024

COL_GATES = 0
COL_SBQ = COL_GATES + N_BRANCH * D_MODEL
COL_SBK = COL_SBQ + SB_WIDTH
COL_SBV = COL_SBK + SB_WIDTH
COL_SGU = COL_SBV + SB_WIDTH
COL_SGV = COL_SGU + SG_WIDTH
COL_GQ = COL_SGV + SG_WIDTH
COL_GK = COL_GQ + GLA_KW
COL_GV = COL_GK + GLA_KW
COL_GR = COL_GV + GLA_VW
COL_GDOWN = COL_GR + GLA_VW
P_WIDTH = COL_GDOWN + LANE

ROW_TILE = 512
FFN_ROW_TILE = 1024
PROJ_N_CHUNK = 512
LOG2E = 1.4426950408889634
PROJ_MAIN_CHUNKS = (
    (0, SB_WIDTH, SB_HEAD_DIM ** -0.5 * LOG2E),
    (SB_WIDTH, SB_WIDTH, None), (2 * SB_WIDTH, SB_WIDTH, None),
    (COL_SGU - COL_SBQ, SG_WIDTH, None), (COL_SGV - COL_SBQ, SG_WIDTH, None),
    (COL_GQ - COL_SBQ, GLA_KW, GLA_DK ** -0.5),
    (COL_GK - COL_SBQ, GLA_KW, None),
    (COL_GV - COL_SBQ, GLA_VW, None), (COL_GR - COL_SBQ, GLA_VW, None),
    (COL_GDOWN - COL_SBQ, LANE, None),
)
SB_BLOCK = 256
SB_LAG = 4
SB_UNROLL = 4
SB_DEAD_LOG2 = -152.0
FFN_F_CHUNK = 256
FFN_HALO = 16


def _params(n_axes):
    return pltpu.CompilerParams(dimension_semantics=("arbitrary",) * n_axes, vmem_limit_bytes=VMEM_LIMIT)


def _resident(shape):
    zeros = (0,) * len(shape)
    return pl.BlockSpec(shape, lambda *_: zeros, pipeline_mode=pl.Buffered(1))


def _layer(shape, layer):
    index = (layer,) + (0,) * len(shape)
    return pl.BlockSpec((None,) + tuple(shape), lambda *_: index, pipeline_mode=pl.Buffered(1))


def _rms(x, g):
    return x * lax.rsqrt(jnp.mean(x * x, axis=-1, keepdims=True) + EPS) * g


def _log_sigmoid_pair(z):
    s = jnp.log(1.0 + jnp.exp(-jnp.abs(z)))
    ls = jnp.minimum(z, 0.0) - s
    return ls, ls - z


def _split_bf16(x):
    hi = x.astype(BF16)
    lo = (x - hi.astype(F32)).astype(BF16)
    return hi, lo


def _dot(a, b):
    return jnp.dot(a, b, preferred_element_type=F32)


def _dot_nt(a, b):
    return lax.dot_general(a, b, (((1,), (1,)), ((), ())), preferred_element_type=F32)


def _dot_tn(a, b):
    return lax.dot_general(a, b, (((0,), (0,)), ((), ())), preferred_element_type=F32)


def _proj_kernel(x_ref, g_ref, wg_ref, wm_ref, o_ref):
    xn = _rms(x_ref[...], g_ref[...]).astype(BF16)
    for c0 in range(0, N_BRANCH * D_MODEL, PROJ_N_CHUNK):
        o_ref[:, c0:c0 + PROJ_N_CHUNK] = _dot(xn, wg_ref[:, c0:c0 + PROJ_N_CHUNK]).astype(o_ref.dtype)
    for c0, width, scale in PROJ_MAIN_CHUNKS:
        acc = _dot(xn, wm_ref[:, c0:c0 + width])
        if scale is not None:
            acc = acc * scale
        o_ref[:, COL_SBQ + c0:COL_SBQ + c0 + width] = acc.astype(o_ref.dtype)


def _proj(h, g, w_gates, w_main, layer):
    t = h.shape[0]
    return pl.pallas_call(
        _proj_kernel,
        out_shape=jax.ShapeDtypeStruct((t, P_WIDTH), BF16),
        grid=(t // ROW_TILE,),
        in_specs=[pl.BlockSpec((ROW_TILE, D_MODEL), lambda i: (i, 0)),
                  _resident((1, D_MODEL)),
                  _layer((D_MODEL, N_BRANCH * D_MODEL), layer),
                  _layer((D_MODEL, P_WIDTH - N_BRANCH * D_MODEL), layer)],
        out_specs=pl.BlockSpec((ROW_TILE, P_WIDTH), lambda i: (i, 0)),
        compiler_params=_params(1),
        name="proj",
    )(h, g, w_gates, w_main)


def _sb_kernel(q_ref, k_ref, v_ref, u_ref, o_ref, qs_ref, z_ref, hl_ref, t_ref, r_ref, rs_ref, acc_ref, c_ref):
    s = q_ref.shape[1]
    tq = SB_BLOCK
    nq = s // tq

    lane_s = lax.broadcasted_iota(jnp.int32, (s, LANE), 1)
    q = q_ref[0]
    qs_ref[0] = jnp.where(lane_s < SB_HEAD_DIM, q, jnp.zeros_like(q))
    qs_ref[1] = jnp.where(lane_s < SB_HEAD_DIM, jnp.zeros_like(q), q)
    @pl.when((pl.program_id(0) == 0) & (pl.program_id(1) == 0))
    def _():
        for ref in (z_ref, hl_ref, t_ref, r_ref, rs_ref, acc_ref, c_ref):
            ref[...] = jnp.zeros_like(ref)

    first_head = lax.broadcasted_iota(jnp.int32, (tq, LANE), 1) < SB_HEAD_DIM
    row = lax.broadcasted_iota(jnp.int32, (2 * tq, tq), 0) & (tq - 1)
    col = lax.broadcasted_iota(jnp.int32, (2 * tq, tq), 1)
    causal = col < row

    def rows(i):
        return pl.ds(pl.multiple_of(i * tq, tq), tq)

    def merge_heads(a):
        return jnp.where(first_head, a[:tq], a[tq:])

    def iteration(it, phase, unit_of, n_units, diag_of):
        slot = [(phase - k) % SB_LAG for k in range(SB_LAG + 1)]
        units = [unit_of(jnp.clip(it - k, 0, n_units - 1)) for k in range(SB_LAG + 1)]
        diag = [diag_of((phase - k) % 2) for k in range(SB_LAG + 1)]
        w = jnp.exp2(z_ref[slot[3]] + t_ref[slot[3]])
        if diag[3]:
            w = jnp.where(causal, w, 0.0)
        r_ref[slot[3]] = _dot(w.astype(BF16), v_ref[0, rows(units[3][1]), :])
        z = z_ref[slot[1]]
        nl = jnp.maximum(z, 0.0) + jnp.log2(1.0 + jnp.exp2(-jnp.abs(z)))
        if diag[1]:
            nl = jnp.where(causal, nl, 0.0)
        hi, lo = _split_bf16(nl)
        hl_ref[slot[1], :, :tq] = hi
        hl_ref[slot[1], :, tq:] = lo
        rs_ref[slot[1]] = jnp.sum(nl, axis=1, keepdims=True)
        t_ref[slot[2]] = _dot(hl_ref[slot[2]], u_ref[...])
        i0, j0, _ = units[0]
        q2 = jnp.concatenate([qs_ref[0, rows(i0), :], qs_ref[1, rows(i0), :]], axis=0)
        z_ref[slot[0]] = _dot_nt(q2, k_ref[0, rows(j0), :])
        i4, _, real = units[4]
        valid = (it >= SB_LAG) & (it < n_units + SB_LAG) & real
        contrib = merge_heads(r_ref[slot[4]])
        rsum = merge_heads(jnp.broadcast_to(rs_ref[slot[4]], (2 * tq, LANE)))
        acc, c = acc_ref[rows(i4), :], c_ref[rows(i4), :]
        if diag[4]:
            acc_new, c_new = contrib, -rsum
        else:
            acc_new, c_new = acc + contrib * jnp.exp2(c), c - rsum
        acc_ref[rows(i4), :] = jnp.where(valid, acc_new, acc)
        c_ref[rows(i4), :] = jnp.where(valid, c_new, c)

    def run(unit_of, n_units, diag_of):
        def body(group, carry):
            for phase in range(SB_UNROLL):
                iteration(group * SB_UNROLL + phase, phase, unit_of, n_units, diag_of)
            return carry

        lax.fori_loop(0, (n_units + SB_LAG + SB_UNROLL - 1) // SB_UNROLL, body, 0)

    def near_unit(u):
        i, odd = u >> 1, u & 1
        return i, jnp.maximum(i - odd, 0), u != 1

    def live_stick(d):
        blk = lax.broadcasted_iota(jnp.int32, (s, LANE), 0) // tq
        return jnp.max(jnp.where(blk >= d, c_ref[...], -jnp.inf))

    run(near_unit, 2 * nq, lambda parity: parity == 0)

    def more(carry):
        d, stick = carry
        return (d < nq) & (stick >= SB_DEAD_LOG2)

    def next_wave(carry):
        d, _ = carry
        run(lambda u: (d + u, u, True), nq - d, lambda parity: False)
        return d + 1, live_stick(d + 1)

    lax.while_loop(more, next_wave, (jnp.int32(2), live_stick(2)))
    o_ref[0] = acc_ref[...].astype(o_ref.dtype)


def _sb_attention(p3, u):
    b, s, _ = p3.shape
    tq = SB_BLOCK
    qb, kb, vb = COL_SBQ // LANE, COL_SBK // LANE, COL_SBV // LANE
    seq_block = lambda col0: pl.BlockSpec((1, s, LANE), lambda bi, hp: (bi, 0, col0 + hp))
    slots = SB_LAG
    return pl.pallas_call(
        _sb_kernel,
        out_shape=jax.ShapeDtypeStruct((b, s, SB_WIDTH), BF16),
        grid=(b, SB_HEADS // 2),
        in_specs=[seq_block(qb), seq_block(kb), seq_block(vb), _resident((2 * tq, tq))],
        out_specs=pl.BlockSpec((1, s, LANE), lambda bi, hp: (bi, 0, hp)),
        scratch_shapes=[pltpu.VMEM((2, s, LANE), BF16),
                        pltpu.VMEM((slots, 2 * tq, tq), F32),
                        pltpu.VMEM((slots, 2 * tq, 2 * tq), BF16),
                        pltpu.VMEM((slots, 2 * tq, tq), F32),
                        pltpu.VMEM((slots, 2 * tq, LANE), F32),
                        pltpu.VMEM((slots, 2 * tq, 1), F32),
                        pltpu.VMEM((s, LANE), F32),
                        pltpu.VMEM((s, LANE), F32)],
        compiler_params=_params(2),
        name="sb_attention",
    )(p3, p3, p3, u)


def _sg_branch(u_ref, v_ref, lng_ref, lnb_ref, w_ref, bias_ref):
    rows = u_ref.shape[0]
    u = jax.nn.gelu(u_ref[...].astype(F32))
    v = jax.nn.gelu(v_ref[...].astype(F32))
    mu = jnp.mean(v, axis=-1, keepdims=True)
    d = v - mu
    var = jnp.mean(d * d, axis=-1, keepdims=True)
    vn = (d * lax.rsqrt(var + EPS) * lng_ref[...] + lnb_ref[...]).astype(BF16)

    wrow = lax.broadcasted_iota(jnp.int32, (SG_CHUNK, 2 * SG_CHUNK), 0)
    wcol = lax.broadcasted_iota(jnp.int32, (SG_CHUNK, 2 * SG_CHUNK), 1) & (SG_CHUNK - 1)
    tril = wcol <= wrow
    ws = [jnp.where(tril, w_ref[p], jnp.zeros((SG_CHUNK, 2 * SG_CHUNK), BF16)) for p in range(SG_GROUPS // 2)]
    lane = lax.broadcasted_iota(jnp.int32, (SG_CHUNK, LANE), 1)
    first = lane < SG_GROUP_DIM
    zero = jnp.zeros((SG_CHUNK, LANE), BF16)
    bias = bias_ref[...]
    out = []
    for c in range(rows // SG_CHUNK):
        r0 = c * SG_CHUNK
        vc = vn[r0:r0 + SG_CHUNK]
        cols = []
        for p in range(SG_GROUPS // 2):
            blk = vc[:, p * LANE:(p + 1) * LANE]
            rhs = jnp.concatenate([jnp.where(first, blk, zero), jnp.where(first, zero, blk)], axis=0)
            cols.append(_dot(ws[p], rhs))
        sp = jnp.concatenate(cols, axis=1) + bias
        out.append((u[r0:r0 + SG_CHUNK] * sp).astype(BF16))
    return jnp.concatenate(out, axis=0)


def _gla_branch(q_ref, k_ref, v_ref, r_ref, dn_ref, wg_ref, bg_ref, ng_ref, lu_ref, state_ref):
    rows = q_ref.shape[0]
    c = GLA_CHUNK
    n_chunks = rows // c
    heads = range(GLA_HEADS)

    gate = _dot(dn_ref[...], wg_ref[...]) + bg_ref[...]
    log_a = _log_sigmoid_pair(gate)[0] * (1.0 / GLA_GATE_NORM)
    hi, lo = _split_bf16(log_a)
    chunk = lambda a, n: a[n * c:(n + 1) * c]

    lu = lu_ref[...]
    sums = [_dot(lu, jnp.concatenate([chunk(hi, n), chunk(lo, n)], axis=0)) for n in range(n_chunks)]
    cum = jnp.concatenate([x[:c] for x in sums], axis=0)
    rem = jnp.concatenate([x[c:] for x in sums], axis=0)
    qf = q_ref[...].astype(F32)
    kf = k_ref[...].astype(F32)
    q_dec = (qf * jnp.exp(cum)).astype(BF16)
    k_inv = (kf * jnp.exp(-cum)).astype(BF16)
    k_end = (kf * jnp.exp(rem)).astype(BF16)

    klane = lax.broadcasted_iota(jnp.int32, (c, GLA_KW), 1) // GLA_DK
    slane = lax.broadcasted_iota(jnp.int32, (GLA_DV, GLA_KW), 1) // GLA_DK
    causal = (lax.broadcasted_iota(jnp.int32, (GLA_HEADS * c, c), 0) & (c - 1)) >= \
        lax.broadcasted_iota(jnp.int32, (GLA_HEADS * c, c), 1)
    zero_q = jnp.zeros((c, GLA_KW), BF16)

    def state_free_part(n):
        qd = chunk(q_dec, n)
        q4 = jnp.concatenate([jnp.where(klane == h, qd, zero_q) for h in heads], axis=0)
        attn = jnp.where(causal, _dot_nt(q4, chunk(k_inv, n)), 0.0).astype(BF16)
        vn = v_ref[n * c:(n + 1) * c, :]
        o_intra = jnp.concatenate(
            [_dot(attn[h * c:(h + 1) * c], vn[:, h * GLA_DV:(h + 1) * GLA_DV]) for h in heads], axis=0)
        kv = _dot_tn(vn, chunk(k_end, n))
        upd = kv[:GLA_DV]
        for h in heads[1:]:
            upd = jnp.where(slane == h, kv[h * GLA_DV:(h + 1) * GLA_DV], upd)
        return q4, o_intra, upd

    ng = ng_ref[...]
    state = state_ref[...]
    parts = [state_free_part(n) for n in range(n_chunks)]
    out = []
    for n in range(n_chunks):
        q4, o_intra, upd = parts[n]
        o = o_intra + _dot_nt(q4, state.astype(BF16))
        o = _rms(o, ng)
        rn = r_ref[n * c:(n + 1) * c, :]
        rh = jnp.concatenate([rn[:, h * GLA_DV:(h + 1) * GLA_DV] for h in heads], axis=0).astype(F32)
        y = (o * (rh * jax.nn.sigmoid(rh))).astype(BF16)
        out.append(jnp.concatenate([y[h * c:(h + 1) * c] for h in heads], axis=1))
        state = state * jnp.exp(cum[(n + 1) * c - 1:(n + 1) * c]) + upd
    state_ref[...] = state
    return jnp.concatenate(out, axis=0)


def _mix_kernel(ya_ref, g_ref, h_ref,
                sgu_ref, sgv_ref, lng_ref, lnb_ref, sgw_ref, sgb_ref,
                gq_ref, gk_ref, gv_ref, gr_ref, gdn_ref, wg_ref, bg_ref, ng_ref, lu_ref,
                pa_ref, pb_ref, pc_ref, wo_ref, pg_ref, o_ref, state_ref, *, steps_per_seq):
    @pl.when(pl.program_id(0) % steps_per_seq == 0)
    def _():
        state_ref[...] = jnp.zeros_like(state_ref)

    gate = lambda n: jax.nn.sigmoid(g_ref[:, n * D_MODEL:(n + 1) * D_MODEL].astype(F32))
    merged = gate(0) * _dot(ya_ref[...], pa_ref[...])
    yb = _sg_branch(sgu_ref, sgv_ref, lng_ref, lnb_ref, sgw_ref, sgb_ref)
    merged = merged + gate(1) * _dot(yb, pb_ref[...])
    yc = _gla_branch(gq_ref, gk_ref, gv_ref, gr_ref, gdn_ref, wg_ref, bg_ref, ng_ref, lu_ref, state_ref)
    merged = merged + gate(2) * _dot(yc, pc_ref[...])
    y = _dot(merged.astype(BF16), wo_ref[...])
    o_ref[...] = h_ref[...] + _rms(y, pg_ref[...])


def _mix(ya, p2, h, sg_ln_g, sg_ln_b, sg_pairs, sg_bias, w_gup, b_gate, norm_g, lu, p_a, p_b, p_c, w_out, post_g,
         layer, seq):
    t = h.shape[0]
    tile = lambda width, col0: pl.BlockSpec((ROW_TILE, width), lambda i: (i, col0 // width))
    kern = functools.partial(_mix_kernel, steps_per_seq=seq // ROW_TILE)
    return pl.pallas_call(
        kern,
        out_shape=jax.ShapeDtypeStruct((t, D_MODEL), F32),
        grid=(t // ROW_TILE,),
        in_specs=[tile(SB_WIDTH, 0), tile(N_BRANCH * D_MODEL, COL_GATES), tile(D_MODEL, 0),
                  tile(SG_WIDTH, COL_SGU), tile(SG_WIDTH, COL_SGV),
                  _resident((1, SG_WIDTH)), _resident((1, SG_WIDTH)),
                  _resident((SG_GROUPS // 2, SG_CHUNK, 2 * SG_CHUNK)), _resident((SG_CHUNK, SG_WIDTH)),
                  tile(GLA_KW, COL_GQ), tile(GLA_KW, COL_GK), tile(GLA_VW, COL_GV), tile(GLA_VW, COL_GR),
                  tile(LANE, COL_GDOWN),
                  _resident((LANE, GLA_KW)), _resident((1, GLA_KW)), _resident((1, GLA_DV)),
                  _resident((2 * GLA_CHUNK, 2 * GLA_CHUNK)),
                  _layer((SB_WIDTH, D_MODEL), layer), _layer((SG_WIDTH, D_MODEL), layer),
                  _layer((GLA_VW, D_MODEL), layer), _layer((D_MODEL, D_MODEL), layer), _resident((1, D_MODEL))],
        out_specs=pl.BlockSpec((ROW_TILE, D_MODEL), lambda i: (i, 0)),
        scratch_shapes=[pltpu.VMEM((GLA_DV, GLA_KW), F32)],
        compiler_params=_params(1),
        name="mix",
    )(ya, p2, h, p2, p2, sg_ln_g, sg_ln_b, sg_pairs, sg_bias, p2, p2, p2, p2, p2, w_gup, b_gate, norm_g, lu,
      p_a, p_b, p_c, w_out, post_g)


def _ffn_kernel(halo_ref, h_ref, g1_ref, wup_ref, cw_ref, cb_ref, wdn_ref, g2_ref, o_ref, act_ref, *, seq):
    tm = h_ref.shape[0]
    g1 = g1_ref[...]
    x = h_ref[...]
    xn = _rms(x, g1).astype(BF16)
    seq_start = (pl.program_id(0) * tm) % seq == 0
    xh = _rms(halo_ref[...], g1).astype(BF16)
    xh = jnp.where(seq_start, jnp.zeros_like(xh), xh)
    xe = jnp.concatenate([xh, xn], axis=0)
    fc = FFN_F_CHUNK
    for c in range(D_FF // fc):
        convs = []
        for half in range(2):
            cols = slice(half * D_FF + fc * c, half * D_FF + fc * (c + 1))
            hid = _dot(xe, wup_ref[:, cols])
            conv = cb_ref[:, cols]
            for j in range(CONV_W):
                off = FFN_HALO - (CONV_W - 1) + j
                conv = conv + hid[off:off + tm] * cw_ref[j:j + 1, cols]
            convs.append(conv)
        act = jax.nn.gelu(convs[0], approximate=True) * convs[1]
        act_ref[:, c * fc:(c + 1) * fc] = act.astype(BF16)
    y = _dot(act_ref[...], wdn_ref[...])
    o_ref[...] = x + _rms(y, g2_ref[...])


def _ffn(h, pre_g, w_up, conv_w, conv_b, w_down, post_g, layer, seq):
    t = h.shape[0]
    halo_blocks = FFN_ROW_TILE // FFN_HALO
    kern = functools.partial(_ffn_kernel, seq=seq)
    return pl.pallas_call(
        kern,
        out_shape=jax.ShapeDtypeStruct((t, D_MODEL), F32),
        grid=(t // FFN_ROW_TILE,),
        in_specs=[pl.BlockSpec((FFN_HALO, D_MODEL), lambda i: (jnp.maximum(i * halo_blocks - 1, 0), 0)),
                  pl.BlockSpec((FFN_ROW_TILE, D_MODEL), lambda i: (i, 0)),
                  _resident((1, D_MODEL)),
                  _layer((D_MODEL, 2 * D_FF), layer),
                  _resident((CONV_W, 2 * D_FF)),
                  _resident((1, 2 * D_FF)),
                  _layer((D_FF, D_MODEL), layer),
                  _resident((1, D_MODEL))],
        out_specs=pl.BlockSpec((FFN_ROW_TILE, D_MODEL), lambda i: (i, 0)),
        scratch_shapes=[pltpu.VMEM((FFN_ROW_TILE, D_FF), BF16)],
        compiler_params=_params(1),
        name="ffn",
    )(h, h, pre_g, w_up, conv_w, conv_b, w_down, post_g)


def kernel(x, mix_pre_g, mix_post_g, w_in, sg_ln_g, sg_ln_b, sg_w, sg_b, gla_w_gup, gla_b_gate, gla_norm_g,
           p_a, p_b, p_c, w_out, ffn_pre_g, ffn_post_g, ffn_w_up, ffn_conv_w, ffn_conv_b, ffn_w_down):
    b, s, d = x.shape
    depth = w_in.shape[0]
    assert d == D_MODEL and s % ROW_TILE == 0 and s % FFN_ROW_TILE == 0 and s % SB_BLOCK == 0 and s >= 2 * SB_BLOCK

    n_main = P_WIDTH - N_BRANCH * D_MODEL - (LANE - GLA_GATE_RANK)
    w_gates = w_in[:, :, n_main:].astype(BF16)
    w_main = jnp.pad(w_in[:, :, :n_main], ((0, 0), (0, 0), (0, LANE - GLA_GATE_RANK))).astype(BF16)
    sg_pairs = sg_w.reshape(depth, SG_GROUPS // 2, 2, SG_CHUNK, SG_CHUNK)
    sg_pairs = jnp.swapaxes(sg_pairs, 2, 3).reshape(depth, SG_GROUPS // 2, SG_CHUNK, 2 * SG_CHUNK).astype(BF16)
    sg_bias = jnp.repeat(jnp.swapaxes(sg_b, 1, 2), SG_GROUP_DIM, axis=2)
    w_gup_p = jnp.pad(gla_w_gup, ((0, 0), (0, LANE - GLA_GATE_RANK), (0, 0))).astype(BF16)
    w_up_b, w_down_b = ffn_w_up.astype(BF16), ffn_w_down.astype(BF16)
    p_a_b, p_b_b, p_c_b, w_out_b = (a.astype(BF16) for a in (p_a, p_b, p_c, w_out))
    tc = jnp.arange(GLA_CHUNK)
    gla_lu = jnp.concatenate([tc[:, None] >= tc[None, :], tc[:, None] < tc[None, :]], axis=0)
    gla_lu = jnp.concatenate([gla_lu, gla_lu], axis=1).astype(BF16)
    tri = jnp.arange(SB_BLOCK)
    u_incl = -(tri[:, None] >= tri[None, :]).astype(BF16)
    u_incl = jnp.concatenate([u_incl, u_incl], axis=0)

    row = lambda a, l: a[l][None, :]
    h = x.reshape(b * s, d)
    for l in range(depth):
        p2 = _proj(h, row(mix_pre_g, l), w_gates, w_main, l)
        ya = _sb_attention(p2.reshape(b, s, P_WIDTH), u_incl).reshape(b * s, SB_WIDTH)
        h = _mix(ya, p2, h, row(sg_ln_g, l), row(sg_ln_b, l), sg_pairs[l], sg_bias[l],
                 w_gup_p[l], row(gla_b_gate, l), row(gla_norm_g, l), gla_lu,
                 p_a_b, p_b_b, p_c_b, w_out_b, row(mix_post_g, l), l, s)
        h = _ffn(h, row(ffn_pre_g, l), w_up_b, ffn_conv_w[l], row(ffn_conv_b, l), w_down_b, row(ffn_post_g, l), l, s)
    return h.reshape(b, s, d)
```
